```python
import jax, jax.numpy as jnp
from jax import lax
import numpy as np

D_MODEL = 2048
BATCH = 16
SEQ = 2048
DEPTH = 1
DEC_BATCH = 2
DEC_SEQ = 8192
PAST_LEN = 128

GRID_W = 64
D_FOURIER = D_MODEL // 2
FOURIER_GROUPS = 4
FOURIER_GW = D_FOURIER // FOURIER_GROUPS
D_NA = D_MODEL - D_FOURIER
NA_HEADS = 16
NA_HEAD_DIM = D_NA // NA_HEADS
NA_KH_MAX = 8
NA_KW = 16
D_MIX = D_FOURIER + D_NA
D_IN_PROJ = D_FOURIER + 3 * D_NA
N_EXPERTS = 16
EC_CAPACITY_FACTOR = 2
EXPERT_FF = D_MODEL
PLE_DIM = 256
LN_EPS = 1e-5
RMS_EPS = 1e-6
DEEPNORM_ALPHA = float((2.0 * DEPTH) ** 0.25)
DEEPNORM_BETA = float((8.0 * DEPTH) ** -0.25)

kernel_name = "hymba_fnet_natten_ec_deepnorm_encoder"


def layer_norm(x, g, b):
    xf = x.astype(jnp.float32)
    mu = jnp.mean(xf, axis=-1, keepdims=True)
    var = jnp.mean(jnp.square(xf - mu), axis=-1, keepdims=True)
    y = (xf - mu) * lax.rsqrt(var + LN_EPS) * g.astype(jnp.float32) + b.astype(jnp.float32)
    return y.astype(x.dtype)


def rms_norm(x, g):
    xf = x.astype(jnp.float32)
    y = xf * lax.rsqrt(jnp.mean(jnp.square(xf), axis=-1, keepdims=True) + RMS_EPS) * g.astype(jnp.float32)
    return y.astype(x.dtype)


def fourier_mix(u, w_fourier):
    B, S, _ = u.shape
    ug = u.reshape(B, S, FOURIER_GROUPS, FOURIER_GW).astype(jnp.float32)
    f = jnp.fft.fft2(ug, axes=(1, 3)).real.astype(u.dtype)
    y = jnp.einsum('bsgc,gcd->bsgd', f, w_fourier)
    return y.reshape(B, S, D_FOURIER)


def neighborhood_attention(q, k, v, rpb):
    B, S, H, dh = q.shape
    rows = S // GRID_W
    kh = min(NA_KH_MAX, rows)
    q = q.reshape(B, rows, GRID_W, H, dh)
    k = k.reshape(B, rows, GRID_W, H, dh)
    v = v.reshape(B, rows, GRID_W, H, dh)
    r = np.arange(rows)
    r_start = np.clip(r - kh // 2, 0, rows - kh)
    row_idx = r_start[:, None] + np.arange(kh)[None, :]
    row_off = row_idx - r[:, None]
    c = np.arange(GRID_W)
    c_start = np.clip(c - NA_KW // 2, 0, GRID_W - NA_KW)
    col_mask = (c[None, :] >= c_start[:, None]) & (c[None, :] < c_start[:, None] + NA_KW)
    col_off = np.clip(c[None, :] - c[:, None], -(NA_KW - 1), NA_KW - 1)
    k_rows = k[:, row_idx]
    v_rows = v[:, row_idx]
    scale = NA_HEAD_DIM ** -0.5
    s = jnp.einsum('brqhd,brikhd->bhrqik', q, k_rows,
                   preferred_element_type=jnp.float32) * scale
    ri = (row_off + NA_KH_MAX - 1)[:, None, :, None]
    ci = (col_off + NA_KW - 1)[None, :, None, :]
    bias = rpb.astype(jnp.float32)[:, ri, ci]
    s = s + bias[None]
    s = jnp.where(col_mask[None, None, None, :, None, :], s, -jnp.inf)
    shp = s.shape
    p = jax.nn.softmax(s.reshape(shp[:4] + (kh * GRID_W,)), axis=-1).reshape(shp)
    o = jnp.einsum('bhrqik,brikhd->brqhd', p.astype(v.dtype), v_rows)
    return o.reshape(B, S, H * dh)


def expert_choice_ffn(x, w_router, w_gate, w_up, w_down):
    B, S, D = x.shape
    n = B * S
    cap = EC_CAPACITY_FACTOR * n // N_EXPERTS
    xt = x.reshape(n, D)
    logits = jnp.einsum('nd,de->ne', xt, w_router, preferred_element_type=jnp.float32)
    aff = jax.nn.softmax(logits, axis=-1)
    gate, idx = lax.top_k(aff.T, cap)
    xe = xt[idx]
    h = jax.nn.silu(jnp.einsum('ecd,edf->ecf', xe, w_gate)) * jnp.einsum('ecd,edf->ecf', xe, w_up)
    ye = jnp.einsum('ecf,efd->ecd', h, w_down) * gate[..., None].astype(x.dtype)
    y = jnp.zeros_like(xt).at[idx.reshape(-1)].add(ye.reshape(-1, D))
    return y.reshape(B, S, D)


def encoder_layer(x, p_l, w_in, w_fourier, rpb, g_fourier, g_na, w_out, ln1_g, ln1_b,
                  w_router, w_gate, w_up, w_down, w_ple_proj, w_ple_gate, ln2_g, ln2_b):
    B, S, _ = x.shape
    proj = jnp.einsum('bsd,de->bse', x, w_in)
    u_f = proj[..., :D_FOURIER]
    q = proj[..., D_FOURIER:D_FOURIER + D_NA].reshape(B, S, NA_HEADS, NA_HEAD_DIM)
    k = proj[..., D_FOURIER + D_NA:D_FOURIER + 2 * D_NA].reshape(B, S, NA_HEADS, NA_HEAD_DIM)
    v = proj[..., D_FOURIER + 2 * D_NA:].reshape(B, S, NA_HEADS, NA_HEAD_DIM)
    y_f = rms_norm(fourier_mix(u_f, w_fourier), g_fourier)
    y_n = rms_norm(neighborhood_attention(q, k, v, rpb), g_na)
    mix = jnp.einsum('bsm,md->bsd', jnp.concatenate([y_f, y_n], axis=-1), w_out)
    x = layer_norm(DEEPNORM_ALPHA * x + mix, ln1_g, ln1_b)
    moe = expert_choice_ffn(x, w_router, w_gate, w_up, w_down)
    ple = jnp.einsum('bsp,pd->bsd', p_l, w_ple_proj) * jax.nn.sigmoid(jnp.einsum('bsd,de->bse', x, w_ple_gate))
    x = layer_norm(DEEPNORM_ALPHA * x + moe + ple, ln2_g, ln2_b)
    return x


def run_trunk(x, p, ln_emb_g, ln_emb_b, w_in, w_fourier, rpb, g_fourier, g_na, w_out, ln1_g, ln1_b,
              w_router, w_gate, w_up, w_down, w_ple_proj, w_ple_gate, ln2_g, ln2_b):
    x = layer_norm(x, ln_emb_g, ln_emb_b)
    for i in range(DEPTH):
        x = encoder_layer(x, p[i], w_in[i], w_fourier[i], rpb[i], g_fourier[i], g_na[i], w_out[i],
                          ln1_g[i], ln1_b[i], w_router[i], w_gate[i], w_up[i], w_down[i],
                          w_ple_proj[i], w_ple_gate[i], ln2_g[i], ln2_b[i])
    return x


def setup_inputs(seed: int = 0) -> dict:
    key = jax.random.key(seed)
    ks = jax.random.split(key, 24)
    f32 = jnp.float32
    nrm = lambda k, shp, s: jax.random.normal(k, shp, f32) * s
    w_in = nrm(ks[6], (DEPTH, D_MODEL, D_IN_PROJ), D_MODEL ** -0.5)
    w_in = w_in.at[..., D_FOURIER + 2 * D_NA:].multiply(DEEPNORM_BETA)
    return {
        "x_prompt": nrm(ks[0], (BATCH, SEQ, D_MODEL), 1.0),
        "x_sample": nrm(ks[1], (DEC_BATCH, DEC_SEQ, D_MODEL), 1.0),
        "p_prompt": nrm(ks[2], (DEPTH, BATCH, SEQ, PLE_DIM), 1.0),
        "p_sample": nrm(ks[3], (DEPTH, DEC_BATCH, DEC_SEQ, PLE_DIM), 1.0),
        "ln_emb_g": 1.0 + nrm(ks[4], (D_MODEL,), 0.02),
        "ln_emb_b": nrm(ks[5], (D_MODEL,), 0.02),
        "w_in": w_in,
        "w_fourier": nrm(ks[7], (DEPTH, FOURIER_GROUPS, FOURIER_GW, FOURIER_GW), FOURIER_GW ** -0.5),
        "rpb": nrm(ks[8], (DEPTH, NA_HEADS, 2 * NA_KH_MAX - 1, 2 * NA_KW - 1), 0.1),
        "g_fourier": 1.0 + nrm(ks[9], (DEPTH, D_FOURIER), 0.02),
        "g_na": 1.0 + nrm(ks[10], (DEPTH, D_NA), 0.02),
        "w_out": nrm(ks[11], (DEPTH, D_MIX, D_MODEL), D_MIX ** -0.5 * DEEPNORM_BETA),
        "ln1_g": 1.0 + nrm(ks[12], (DEPTH, D_MODEL), 0.02),
        "ln1_b": nrm(ks[13], (DEPTH, D_MODEL), 0.02),
        "w_router": nrm(ks[14], (DEPTH, D_MODEL, N_EXPERTS), D_MODEL ** -0.5),
        "w_gate": nrm(ks[15], (DEPTH, N_EXPERTS, D_MODEL, EXPERT_FF), D_MODEL ** -0.5),
        "w_up": nrm(ks[16], (DEPTH, N_EXPERTS, D_MODEL, EXPERT_FF), D_MODEL ** -0.5),
        "w_down": nrm(ks[17], (DEPTH, N_EXPERTS, EXPERT_FF, D_MODEL), EXPERT_FF ** -0.5 * DEEPNORM_BETA),
        "w_ple_proj": nrm(ks[18], (DEPTH, PLE_DIM, D_MODEL), PLE_DIM ** -0.5 * DEEPNORM_BETA),
        "w_ple_gate": nrm(ks[19], (DEPTH, D_MODEL, D_MODEL), D_MODEL ** -0.5),
        "ln2_g": 1.0 + nrm(ks[20], (DEPTH, D_MODEL), 0.02),
        "ln2_b": nrm(ks[21], (DEPTH, D_MODEL), 0.02),
    }


def reference(x_prompt, x_sample, p_prompt, p_sample, ln_emb_g, ln_emb_b, w_in, w_fourier, rpb,
              g_fourier, g_na, w_out, ln1_g, ln1_b, w_router, w_gate, w_up, w_down,
              w_ple_proj, w_ple_gate, ln2_g, ln2_b):
    y_prompt = run_trunk(x_prompt, p_prompt, ln_emb_g, ln_emb_b, w_in, w_fourier, rpb, g_fourier, g_na,
                         w_out, ln1_g, ln1_b, w_router, w_gate, w_up, w_down, w_ple_proj, w_ple_gate,
                         ln2_g, ln2_b)
    y_sample = run_trunk(x_sample, p_sample, ln_emb_g, ln_emb_b, w_in, w_fourier, rpb, g_fourier, g_na,
                         w_out, ln1_g, ln1_b, w_router, w_gate, w_up, w_down, w_ple_proj, w_ple_gate,
                         ln2_g, ln2_b)
    return (y_prompt, y_sample)
```

```python
import functools

import numpy as np
import jax
import jax.numpy as jnp
from jax import lax
from jax.experimental import pallas as pl
from jax.experimental.pallas import tpu as pltpu

D_MODEL = 2048
GRID_W = 64
D_FOURIER = 1024
FOURIER_GROUPS = 4
FOURIER_GW = 256
D_NA = 1024
NA_HEADS = 16
NA_HEAD_DIM = 64
NA_KH = 8
NA_KW = 16
N_EXPERTS = 16
EC_CAPACITY_FACTOR = 2
PLE_DIM = 256
LN_EPS = 1e-5
RMS_EPS = 1e-6
ALPHA = float(2.0 ** 0.25)
D_PROJ = 2 * D_FOURIER + 3 * D_NA
MASK_VALUE = -1e30
VMEM_LIMIT = 56 * 1024 * 1024
FFT_S1 = 128

BF16 = jnp.bfloat16
F32 = jnp.float32


def _cparams(*sem):
    return pltpu.CompilerParams(dimension_semantics=sem, vmem_limit_bytes=VMEM_LIMIT)


def _layer_norm(x, g, b):
    mu = jnp.mean(x, axis=-1, keepdims=True)
    xc = x - mu
    var = jnp.mean(xc * xc, axis=-1, keepdims=True)
    return xc * lax.rsqrt(var + LN_EPS) * g + b


def _rms_norm(x, g):
    return x * lax.rsqrt(jnp.mean(x * x, axis=-1, keepdims=True) + RMS_EPS) * g


def _fold_kernel(win_ref, wf_ref, cc_ref, sc_ref, zr_ref, zi_ref):
    hi = lax.Precision.HIGHEST
    wf = wf_ref[...]
    a = jnp.dot(cc_ref[...], wf, precision=hi, preferred_element_type=F32)
    b = jnp.dot(sc_ref[...], wf, precision=hi, preferred_element_type=F32)
    w = win_ref[...]
    zr_ref[...] = jnp.dot(w, a, precision=hi, preferred_element_type=F32).astype(BF16)
    zi_ref[...] = (-jnp.dot(w, b, precision=hi, preferred_element_type=F32)).astype(BF16)


def _fold_fourier(w_in, w_fourier):
    c = np.arange(FOURIER_GW)
    ang = 2.0 * np.pi * ((c[:, None] * c[None, :]) % FOURIER_GW) / FOURIER_GW
    cc = jnp.asarray(np.cos(ang), F32)
    sc = jnp.asarray(np.sin(ang), F32)
    gw = FOURIER_GW
    return pl.pallas_call(
        _fold_kernel,
        grid=(FOURIER_GROUPS,),
        in_specs=[
            pl.BlockSpec((D_MODEL, gw), lambda g: (0, g)),
            pl.BlockSpec((None, gw, gw), lambda g: (g, 0, 0)),
            pl.BlockSpec((gw, gw), lambda g: (0, 0)),
            pl.BlockSpec((gw, gw), lambda g: (0, 0)),
        ],
        out_specs=[
            pl.BlockSpec((D_MODEL, gw), lambda g: (0, g)),
            pl.BlockSpec((D_MODEL, gw), lambda g: (0, g)),
        ],
        out_shape=[jax.ShapeDtypeStruct((D_MODEL, D_FOURIER), BF16)] * 2,
        compiler_params=_cparams("arbitrary"),
        name="fold_fourier",
    )(w_in, w_fourier, cc, sc)


def _ln_inproj_kernel(x_ref, g_ref, b_ref, w_ref, x0_ref, proj_ref, xn_ref):
    @pl.when(pl.program_id(1) == 0)
    def _():
        y = _layer_norm(x_ref[...], g_ref[...], b_ref[...])
        x0_ref[...] = y
        xn_ref[...] = y.astype(BF16)

    proj_ref[...] = jnp.dot(xn_ref[...], w_ref[...], preferred_element_type=F32).astype(BF16)


def _ln_inproj(x, g, b, w):
    n = x.shape[0]
    tm = min(512, n)
    tn = 1024
    return pl.pallas_call(
        _ln_inproj_kernel,
        grid=(n // tm, D_PROJ // tn),
        in_specs=[
            pl.BlockSpec((tm, D_MODEL), lambda i, j: (i, 0)),
            pl.BlockSpec((1, D_MODEL), lambda i, j: (0, 0)),
            pl.BlockSpec((1, D_MODEL), lambda i, j: (0, 0)),
            pl.BlockSpec((D_MODEL, tn), lambda i, j: (0, j)),
        ],
        out_specs=[
            pl.BlockSpec((tm, D_MODEL), lambda i, j: (i, 0)),
            pl.BlockSpec((tm, tn), lambda i, j: (i, j)),
        ],
        out_shape=[
            jax.ShapeDtypeStruct((n, D_MODEL), F32),
            jax.ShapeDtypeStruct((n, D_PROJ), BF16),
        ],
        scratch_shapes=[pltpu.VMEM((tm, D_MODEL), BF16)],
        compiler_params=_cparams("parallel", "arbitrary"),
        name="ln_inproj",
    )(x, g, b, w)


def _fft1_kernel(z_ref, m_ref, y_ref, *, tb):
    for t in range(tb):
        x = jnp.concatenate([z_ref[t, :, :D_FOURIER], z_ref[t, :, D_FOURIER:]], axis=0)
        y = jnp.dot(m_ref[t], x, preferred_element_type=F32)
        s1 = y.shape[0] // 2
        y_ref[t, :, :D_FOURIER] = y[:s1].astype(BF16)
        y_ref[t, :, D_FOURIER:] = y[s1:].astype(BF16)


def _fft1_matrices(s1, s2):
    s = s1 * s2
    k1 = np.arange(s1)[None, :, None]
    n = (s2 * np.arange(s1)[None, None, :] + np.arange(s2)[:, None, None])
    ang = 2.0 * np.pi * ((k1 * n) % s) / s
    fr, fi = np.cos(ang), -np.sin(ang)
    m = np.concatenate([np.concatenate([fr, -fi], axis=2),
                        np.concatenate([fi, fr], axis=2)], axis=1)
    return jnp.asarray(m, F32).astype(BF16)


def _fft1(zt, s1, s2):
    b = zt.shape[0]
    tb = min(4, s2)
    m = _fft1_matrices(s1, s2)
    return pl.pallas_call(
        functools.partial(_fft1_kernel, tb=tb),
        grid=(s2 // tb, b),
        in_specs=[
            pl.BlockSpec((None, tb, s1, 2 * D_FOURIER), lambda j, i: (i, j, 0, 0)),
            pl.BlockSpec((tb, 2 * s1, 2 * s1), lambda j, i: (j, 0, 0)),
        ],
        out_specs=pl.BlockSpec((None, tb, s1, 2 * D_FOURIER), lambda j, i: (i, j, 0, 0)),
        out_shape=jax.ShapeDtypeStruct(zt.shape, BF16),
        compiler_params=_cparams("parallel", "parallel"),
        name="fft_stage1",
    )(zt, m)


def _fft2_kernel(v_ref, m_ref, g_ref, o_ref, *, tb):
    m = m_ref[...]
    g = g_ref[...]
    for t in range(tb):
        x = jnp.concatenate([v_ref[t, :, :D_FOURIER], v_ref[t, :, D_FOURIER:]], axis=0)
        y = jnp.dot(m, x, preferred_element_type=F32)
        o_ref[t] = _rms_norm(y, g).astype(BF16)


def _fft2(v, g_fourier, s1, s2):
    b = v.shape[0]
    tb = 8
    k2 = np.arange(s2)
    ang = 2.0 * np.pi * ((k2[:, None] * k2[None, :]) % s2) / s2
    m = jnp.asarray(np.concatenate([np.cos(ang), np.sin(ang)], axis=1), F32).astype(BF16)
    return pl.pallas_call(
        functools.partial(_fft2_kernel, tb=tb),
        grid=(b, s1 // tb),
        in_specs=[
            pl.BlockSpec((None, tb, s2, 2 * D_FOURIER), lambda i, j: (i, j, 0, 0)),
            pl.BlockSpec((s2, 2 * s2), lambda i, j: (0, 0)),
            pl.BlockSpec((1, D_FOURIER), lambda i, j: (0, 0)),
        ],
        out_specs=pl.BlockSpec((None, tb, s2, D_FOURIER), lambda i, j: (i, j, 0, 0)),
        out_shape=jax.ShapeDtypeStruct((b, s1, s2, D_FOURIER), BF16),
        compiler_params=_cparams("parallel", "parallel"),
        name="fft_stage2",
    )(v, m, g_fourier)


def _fourier_branch(proj, g_fourier, b, s):
    s1, s2 = FFT_S1, s // FFT_S1
    z = proj[:, :2 * D_FOURIER].reshape(b, s1, s2, 2 * D_FOURIER)
    y = _fft1(jnp.transpose(z, (0, 2, 1, 3)), s1, s2)
    o = _fft2(jnp.transpose(y, (0, 2, 1, 3)), g_fourier, s1, s2)
    return jnp.transpose(o, (0, 2, 1, 3)).reshape(b * s, D_FOURIER)


def _na_kernel(q_ref, k_ref, v_ref, bias_ref, o_ref, *, rows):
    kh = min(NA_KH, rows)
    nk = kh * GRID_W
    lane = lax.broadcasted_iota(jnp.int32, (GRID_W, 2 * NA_HEAD_DIM), 1)
    first = lane < NA_HEAD_DIM
    head0 = first.astype(F32).astype(BF16)
    head1 = (1.0 - first.astype(F32)).astype(BF16)
    scale = NA_HEAD_DIM ** -0.5

    def body(r, carry):
        rs = jnp.clip(r - kh // 2, 0, rows - kh)
        pat = r - rs
        q = q_ref[pl.ds(pl.multiple_of(r * GRID_W, GRID_W), GRID_W), :]
        kw = k_ref[pl.ds(pl.multiple_of(rs * GRID_W, GRID_W), nk), :]
        vw = v_ref[pl.ds(pl.multiple_of(rs * GRID_W, GRID_W), nk), :]
        outs = []
        for h in range(2):
            qh = q * (head0 if h == 0 else head1)
            s = lax.dot_general(qh, kw, (((1,), (1,)), ((), ())), preferred_element_type=F32)
            s = s * scale + bias_ref[h, pat]
            m = jnp.max(s, axis=-1, keepdims=True)
            p = jnp.exp(s - m)
            l = jnp.sum(p, axis=-1, keepdims=True)
            o = jnp.dot(p.astype(BF16), vw, preferred_element_type=F32)
            outs.append(o / l)
        o_ref[pl.ds(pl.multiple_of(r * GRID_W, GRID_W), GRID_W), :] = jnp.where(
            first, outs[0], outs[1]).astype(BF16)
        return carry

    lax.fori_loop(0, rows, body, 0)


def _na_bias_table(rpb, rows):
    kh = min(NA_KH, rows)
    c = np.arange(GRID_W)
    c_start = np.clip(c - NA_KW // 2, 0, GRID_W - NA_KW)
    col_mask = (c[None, :] >= c_start[:, None]) & (c[None, :] < c_start[:, None] + NA_KW)
    col_off = np.clip(c[None, :] - c[:, None], -(NA_KW - 1), NA_KW - 1) + NA_KW - 1
    pat = np.arange(kh)
    row_off = np.arange(kh)[None, :] - pat[:, None] + NA_KH - 1
    bias = rpb.astype(F32)[:, row_off[:, None, :, None], col_off[None, :, None, :]]
    bias = jnp.where(col_mask[None, None, :, None, :], bias, MASK_VALUE)
    return bias.reshape(NA_HEADS, kh, GRID_W, kh * GRID_W)


def _na_attention(proj3, bias, b, s):
    rows = s // GRID_W
    kh = min(NA_KH, rows)
    hp = NA_HEADS // 2
    q0 = 2 * D_FOURIER // 128
    blk = lambda off: pl.BlockSpec((None, s, 128), lambda h, i: (i, 0, off + h))
    return pl.pallas_call(
        functools.partial(_na_kernel, rows=rows),
        grid=(hp, b),
        in_specs=[
            blk(q0), blk(q0 + hp), blk(q0 + 2 * hp),
            pl.BlockSpec((2, kh, GRID_W, kh * GRID_W), lambda h, i: (h, 0, 0, 0)),
        ],
        out_specs=pl.BlockSpec((None, s, 128), lambda h, i: (i, 0, h)),
        out_shape=jax.ShapeDtypeStruct((b, s, D_NA), BF16),
        compiler_params=_cparams("parallel", "parallel"),
        name="na_attention",
    )(proj3, proj3, proj3, bias)


def _outproj_kernel(yf_ref, yn_ref, x0_ref, gna_ref, w_ref, g_ref, b_ref, wr_ref,
                    x1_ref, x1b_ref, aff_ref):
    yn = _rms_norm(yn_ref[...].astype(F32), gna_ref[...]).astype(BF16)
    mix = jnp.dot(yf_ref[...], w_ref[:D_FOURIER, :], preferred_element_type=F32)
    mix += jnp.dot(yn, w_ref[D_FOURIER:, :], preferred_element_type=F32)
    x1 = _layer_norm(ALPHA * x0_ref[...] + mix, g_ref[...], b_ref[...])
    x1_ref[...] = x1
    x1b = x1.astype(BF16)
    x1b_ref[...] = x1b
    logits = jnp.dot(x1b, wr_ref[...], preferred_element_type=F32)
    lane = lax.broadcasted_iota(jnp.int32, logits.shape, 1)
    logits = jnp.where(lane < N_EXPERTS, logits, MASK_VALUE)
    m = jnp.max(logits, axis=-1, keepdims=True)
    e = jnp.exp(logits - m)
    aff_ref[...] = e / jnp.sum(e, axis=-1, keepdims=True)


def _outproj(yf, yn, x0, g_na, w_out, ln_g, ln_b, w_router):
    n = x0.shape[0]
    tm = min(256, n)
    row = lambda w: pl.BlockSpec((tm, w), lambda i: (i, 0))
    full = lambda r, c: pl.BlockSpec((r, c), lambda i: (0, 0))
    return pl.pallas_call(
        _outproj_kernel,
        grid=(n // tm,),
        in_specs=[row(D_FOURIER), row(D_NA), row(D_MODEL), full(1, D_NA),
                  full(D_MODEL, D_MODEL), full(1, D_MODEL), full(1, D_MODEL), full(D_MODEL, 128)],
        out_specs=[row(D_MODEL), row(D_MODEL), row(128)],
        out_shape=[jax.ShapeDtypeStruct((n, D_MODEL), F32),
                   jax.ShapeDtypeStruct((n, D_MODEL), BF16),
                   jax.ShapeDtypeStruct((n, 128), F32)],
        compiler_params=_cparams("parallel"),
        name="outproj_ln_router",
    )(yf, yn, x0, g_na, w_out, ln_g, ln_b, w_router)


def _expert_kernel(x_ref, gate_ref, wg_ref, wu_ref, wd_ref, o_ref):
    f = pl.program_id(2)
    x = x_ref[...]
    g = jnp.dot(x, wg_ref[...], preferred_element_type=F32)
    u = jnp.dot(x, wu_ref[...], preferred_element_type=F32)
    h = (g * jax.nn.sigmoid(g) * u).astype(BF16)
    part = jnp.dot(h, wd_ref[...], preferred_element_type=F32)

    @pl.when(f == 0)
    def _():
        o_ref[...] = part

    @pl.when(f != 0)
    def _():
        o_ref[...] += part

    @pl.when(f == pl.num_programs(2) - 1)
    def _():
        o_ref[...] *= gate_ref[...]


def _experts(xe, gate, w_gate, w_up, w_down):
    e, cap, d = xe.shape
    ff = w_gate.shape[-1]
    tm = min(1024, cap)
    tf = 256
    return pl.pallas_call(
        _expert_kernel,
        grid=(e, cap // tm, ff // tf),
        in_specs=[
            pl.BlockSpec((None, tm, d), lambda i, j, f: (i, j, 0)),
            pl.BlockSpec((None, tm, 1), lambda i, j, f: (i, j, 0)),
            pl.BlockSpec((None, d, tf), lambda i, j, f: (i, 0, f)),
            pl.BlockSpec((None, d, tf), lambda i, j, f: (i, 0, f)),
            pl.BlockSpec((None, tf, d), lambda i, j, f: (i, f, 0)),
        ],
        out_specs=pl.BlockSpec((None, tm, d), lambda i, j, f: (i, j, 0)),
        out_shape=jax.ShapeDtypeStruct((e, cap, d), F32),
        compiler_params=_cparams("parallel", "parallel", "arbitrary"),
        name="expert_ffn",
    )(xe, gate, w_gate, w_up, w_down)


def _final_kernel(x1_ref, x1b_ref, moe_ref, p_ref, wpp_ref, wpg_ref, g_ref, b_ref, o_ref):
    pp = jnp.dot(p_ref[...].astype(BF16), wpp_ref[...], preferred_element_type=F32)
    pg = jnp.dot(x1b_ref[...], wpg_ref[...], preferred_element_type=F32)
    y = ALPHA * x1_ref[...] + moe_ref[...] + pp * jax.nn.sigmoid(pg)
    o_ref[...] = _layer_norm(y, g_ref[...], b_ref[...])


def _final(x1, x1b, moe, p, w_pp, w_pg, ln_g, ln_b):
    n = x1.shape[0]
    tm = min(256, n)
    row = lambda w: pl.BlockSpec((tm, w), lambda i: (i, 0))
    full = lambda r, c: pl.BlockSpec((r, c), lambda i: (0, 0))
    return pl.pallas_call(
        _final_kernel,
        grid=(n // tm,),
        in_specs=[row(D_MODEL), row(D_MODEL), row(D_MODEL), row(PLE_DIM),
                  full(PLE_DIM, D_MODEL), full(D_MODEL, D_MODEL), full(1, D_MODEL), full(1, D_MODEL)],
        out_specs=row(D_MODEL),
        out_shape=jax.ShapeDtypeStruct((n, D_MODEL), F32),
        compiler_params=_cparams("parallel"),
        name="ple_final_ln",
    )(x1, x1b, moe, p, w_pp, w_pg, ln_g, ln_b)


def _trunk(x, p, wts):
    b, s, _ = x.shape
    n = b * s
    x0, proj = _ln_inproj(x.reshape(n, D_MODEL), wts["ln_emb_g"], wts["ln_emb_b"], wts["w_proj"])
    yf = _fourier_branch(proj, wts["g_fourier"], b, s)
    bias = _na_bias_table(wts["rpb"], s // GRID_W)
    yn = _na_attention(proj.reshape(b, s, D_PROJ), bias, b, s).reshape(n, D_NA)
    x1, x1b, aff = _outproj(yf, yn, x0, wts["g_na"], wts["w_out"], wts["ln1_g"], wts["ln1_b"],
                            wts["w_router"])
    cap = EC_CAPACITY_FACTOR * n // N_EXPERTS
    gate, idx = lax.top_k(aff[:, :N_EXPERTS].T, cap)
    xe = x1b[idx]
    ye = _experts(xe, gate[..., None], wts["w_gate"], wts["w_up"], wts["w_down"])
    moe = jnp.zeros((n, D_MODEL), F32).at[idx.reshape(-1)].add(ye.reshape(-1, D_MODEL))
    y = _final(x1, x1b, moe, p.reshape(n, PLE_DIM), wts["w_ple_proj"], wts["w_ple_gate"],
               wts["ln2_g"], wts["ln2_b"])
    return y.reshape(b, s, D_MODEL)


def kernel(x_prompt, x_sample, p_prompt, p_sample, ln_emb_g, ln_emb_b, w_in, w_fourier, rpb,
           g_fourier, g_na, w_out, ln1_g, ln1_b, w_router, w_gate, w_up, w_down,
           w_ple_proj, w_ple_gate, ln2_g, ln2_b):
    assert w_in.shape[0] == 1, "one encoder layer"
    w_zr, w_zi = _fold_fourier(w_in[0], w_fourier[0])
    row = lambda a: a.reshape(1, -1).astype(F32)
    wts = {
        "ln_emb_g": row(ln_emb_g), "ln_emb_b": row(ln_emb_b),
        "w_proj": jnp.concatenate([w_zr, w_zi, w_in[0][:, D_FOURIER:].astype(BF16)], axis=1),
        "rpb": rpb[0],
        "g_fourier": row(g_fourier[0]), "g_na": row(g_na[0]),
        "w_out": w_out[0].astype(BF16),
        "ln1_g": row(ln1_g[0]), "ln1_b": row(ln1_b[0]),
        "w_router": jnp.pad(w_router[0], ((0, 0), (0, 128 - N_EXPERTS))).astype(BF16),
        "w_gate": w_gate[0].astype(BF16), "w_up": w_up[0].astype(BF16),
        "w_down": w_down[0].astype(BF16),
        "w_ple_proj": w_ple_proj[0].astype(BF16), "w_ple_gate": w_ple_gate[0].astype(BF16),
        "ln2_g": row(ln2_g[0]), "ln2_b": row(ln2_b[0]),
    }
    return (_trunk(x_prompt, p_prompt[0], wts), _trunk(x_sample, p_sample[0], wts))
```

```python
import functools

import numpy as np
import jax
import jax.numpy as jnp
from jax import lax
from jax.experimental import pallas as pl
from jax.experimental.pallas import tpu as pltpu

D_MODEL = 2048
GRID_W = 64
D_FOURIER = 1024
FOURIER_GROUPS = 4
FOURIER_GW = 256
D_NA = 1024
NA_HEADS = 16
NA_HEAD_DIM = 64
NA_KH = 8
NA_KW = 16
N_EXPERTS = 16
EC_CAPACITY_FACTOR = 2
PLE_DIM = 256
LN_EPS = 1e-5
RMS_EPS = 1e-6
ALPHA = float(2.0 ** 0.25)
D_PROJ = 2 * D_FOURIER + 3 * D_NA
MASK_VALUE = -1e30
VMEM_LIMIT = 56 * 1024 * 1024
FFT_S1 = 128

BF16 = jnp.bfloat16
F32 = jnp.float32


def _cparams(*sem):
    return pltpu.CompilerParams(dimension_semantics=sem, vmem_limit_bytes=VMEM_LIMIT)


def _resident(shape):
    return pl.BlockSpec(shape, lambda *_: (0,) * len(shape), pipeline_mode=pl.Buffered(1))


def _layer_norm(x, g, b):
    mu = jnp.mean(x, axis=-1, keepdims=True)
    xc = x - mu
    var = jnp.mean(xc * xc, axis=-1, keepdims=True)
    return xc * lax.rsqrt(var + LN_EPS) * g + b


def _rms_norm(x, g):
    return x * lax.rsqrt(jnp.mean(x * x, axis=-1, keepdims=True) + RMS_EPS) * g


def _fold_kernel(win_ref, wf_ref, cc_ref, sc_ref, zr_ref, zi_ref):
    hi = lax.Precision.HIGHEST
    wf = wf_ref[...]
    a = jnp.dot(cc_ref[...], wf, precision=hi, preferred_element_type=F32)
    b = jnp.dot(sc_ref[...], wf, precision=hi, preferred_element_type=F32)
    w = win_ref[...]
    zr_ref[...] = jnp.dot(w, a, precision=hi, preferred_element_type=F32).astype(BF16)
    zi_ref[...] = (-jnp.dot(w, b, precision=hi, preferred_element_type=F32)).astype(BF16)


def _fold_fourier(w_in, w_fourier):
    c = np.arange(FOURIER_GW)
    ang = 2.0 * np.pi * ((c[:, None] * c[None, :]) % FOURIER_GW) / FOURIER_GW
    cc = jnp.asarray(np.cos(ang), F32)
    sc = jnp.asarray(np.sin(ang), F32)
    gw = FOURIER_GW
    return pl.pallas_call(
        _fold_kernel,
        grid=(FOURIER_GROUPS,),
        in_specs=[
            pl.BlockSpec((D_MODEL, gw), lambda g: (0, g)),
            pl.BlockSpec((None, gw, gw), lambda g: (g, 0, 0)),
            pl.BlockSpec((gw, gw), lambda g: (0, 0)),
            pl.BlockSpec((gw, gw), lambda g: (0, 0)),
        ],
        out_specs=[
            pl.BlockSpec((D_MODEL, gw), lambda g: (0, g)),
            pl.BlockSpec((D_MODEL, gw), lambda g: (0, g)),
        ],
        out_shape=[jax.ShapeDtypeStruct((D_MODEL, D_FOURIER), BF16)] * 2,
        compiler_params=_cparams("arbitrary"),
        name="fold_fourier",
    )(w_in, w_fourier, cc, sc)


def _ln_inproj_kernel(x_ref, g_ref, b_ref, w_ref, proj_ref, xn_ref):
    @pl.when(pl.program_id(1) == 0)
    def _():
        xn_ref[...] = _layer_norm(x_ref[...], g_ref[...], b_ref[...]).astype(BF16)

    proj_ref[...] = jnp.dot(xn_ref[...], w_ref[...], preferred_element_type=F32).astype(BF16)


def _ln_inproj(x, g, b, w):
    n = x.shape[0]
    tm = min(1024, n)
    tn = 1024
    return pl.pallas_call(
        _ln_inproj_kernel,
        grid=(n // tm, D_PROJ // tn),
        in_specs=[
            pl.BlockSpec((tm, D_MODEL), lambda i, j: (i, 0)),
            pl.BlockSpec((1, D_MODEL), lambda i, j: (0, 0)),
            pl.BlockSpec((1, D_MODEL), lambda i, j: (0, 0)),
            pl.BlockSpec((D_MODEL, tn), lambda i, j: (0, j)),
        ],
        out_specs=pl.BlockSpec((tm, tn), lambda i, j: (i, j)),
        out_shape=jax.ShapeDtypeStruct((n, D_PROJ), BF16),
        scratch_shapes=[pltpu.VMEM((tm, D_MODEL), BF16)],
        compiler_params=_cparams("parallel", "arbitrary"),
        name="ln_inproj",
    )(x, g, b, w)


def _fft1_kernel(z_ref, m_ref, y_ref, *, tb):
    for t in range(tb):
        x = jnp.concatenate([z_ref[t, :, :D_FOURIER], z_ref[t, :, D_FOURIER:]], axis=0)
        y = jnp.dot(m_ref[t], x, preferred_element_type=F32)
        s1 = y.shape[0] // 2
        y_ref[t, :, :D_FOURIER] = y[:s1].astype(BF16)
        y_ref[t, :, D_FOURIER:] = y[s1:].astype(BF16)


def _fft1_matrices(s1, s2):
    s = s1 * s2
    k1 = np.arange(s1)[None, :, None]
    n = (s2 * np.arange(s1)[None, None, :] + np.arange(s2)[:, None, None])
    ang = 2.0 * np.pi * ((k1 * n) % s) / s
    fr, fi = np.cos(ang), -np.sin(ang)
    m = np.concatenate([np.concatenate([fr, -fi], axis=2),
                        np.concatenate([fi, fr], axis=2)], axis=1)
    return jnp.asarray(m, F32).astype(BF16)


def _fft1(zt, s1, s2):
    b = zt.shape[0]
    tb = min(4, s2)
    m = _fft1_matrices(s1, s2)
    return pl.pallas_call(
        functools.partial(_fft1_kernel, tb=tb),
        grid=(s2 // tb, b),
        in_specs=[
            pl.BlockSpec((None, tb, s1, 2 * D_FOURIER), lambda j, i: (i, j, 0, 0)),
            pl.BlockSpec((tb, 2 * s1, 2 * s1), lambda j, i: (j, 0, 0)),
        ],
        out_specs=pl.BlockSpec((None, tb, s1, 2 * D_FOURIER), lambda j, i: (i, j, 0, 0)),
        out_shape=jax.ShapeDtypeStruct(zt.shape, BF16),
        compiler_params=_cparams("parallel", "parallel"),
        name="fft_stage1",
    )(zt, m)


def _fft2_kernel(v_ref, m_ref, g_ref, o_ref, *, tb):
    m = m_ref[...]
    g = g_ref[...]
    for t in range(tb):
        x = jnp.concatenate([v_ref[t, :, :D_FOURIER], v_ref[t, :, D_FOURIER:]], axis=0)
        y = jnp.dot(m, x, preferred_element_type=F32)
        o_ref[t] = _rms_norm(y, g).astype(BF16)


def _fft2(v, g_fourier, s1, s2):
    b = v.shape[0]
    tb = 8
    k2 = np.arange(s2)
    ang = 2.0 * np.pi * ((k2[:, None] * k2[None, :]) % s2) / s2
    m = jnp.asarray(np.concatenate([np.cos(ang), np.sin(ang)], axis=1), F32).astype(BF16)
    return pl.pallas_call(
        functools.partial(_fft2_kernel, tb=tb),
        grid=(b, s1 // tb),
        in_specs=[
            pl.BlockSpec((None, tb, s2, 2 * D_FOURIER), lambda i, j: (i, j, 0, 0)),
            pl.BlockSpec((s2, 2 * s2), lambda i, j: (0, 0)),
            pl.BlockSpec((1, D_FOURIER), lambda i, j: (0, 0)),
        ],
        out_specs=pl.BlockSpec((None, tb, s2, D_FOURIER), lambda i, j: (i, j, 0, 0)),
        out_shape=jax.ShapeDtypeStruct((b, s1, s2, D_FOURIER), BF16),
        compiler_params=_cparams("parallel", "parallel"),
        name="fft_stage2",
    )(v, m, g_fourier)


def _fourier_branch(proj, g_fourier, b, s):
    s1, s2 = FFT_S1, s // FFT_S1
    z = proj[:, :2 * D_FOURIER].reshape(b, s1, s2, 2 * D_FOURIER)
    y = _fft1(jnp.transpose(z, (0, 2, 1, 3)), s1, s2)
    o = _fft2(jnp.transpose(y, (0, 2, 1, 3)), g_fourier, s1, s2)
    return jnp.transpose(o, (0, 2, 1, 3)).reshape(b * s, D_FOURIER)


NA_ROWS_PER_STEP = 4


def _na_kernel(q_ref, k_ref, v_ref, bias_ref, o_ref, *, rows):
    nk = NA_KH * GRID_W
    lane = lax.broadcasted_iota(jnp.int32, (GRID_W, 2 * NA_HEAD_DIM), 1)
    first = lane < NA_HEAD_DIM
    scale = NA_HEAD_DIM ** -0.5
    head0 = (first.astype(F32) * scale).astype(BF16)
    head1 = ((1.0 - first.astype(F32)) * scale).astype(BF16)

    def body(i, carry):
        rr = [i * NA_ROWS_PER_STEP + u for u in range(NA_ROWS_PER_STEP)]
        rs = [jnp.clip(r - NA_KH // 2, 0, rows - NA_KH) for r in rr]
        tok = lambda r: pl.ds(pl.multiple_of(r * GRID_W, GRID_W), GRID_W)
        win = lambda r: pl.ds(pl.multiple_of(r * GRID_W, GRID_W), nk)
        scores = []
        for r, r0 in zip(rr, rs):
            q = q_ref[tok(r), :]
            qbd = jnp.concatenate([q * head0, q * head1], axis=0)
            s = lax.dot_general(qbd, k_ref[win(r0), :], (((1,), (1,)), ((), ())),
                                preferred_element_type=F32)
            scores.append(s + bias_ref[r - r0])
        probs = []
        for s in scores:
            p = jnp.exp(s - jnp.max(s, axis=-1, keepdims=True))
            probs.append((p.astype(BF16), jnp.sum(p, axis=-1, keepdims=True)))
        for r, r0, (p, l) in zip(rr, rs, probs):
            o = jnp.dot(p, v_ref[win(r0), :], preferred_element_type=F32) / l
            o_ref[tok(r), :] = jnp.where(first, o[:GRID_W], o[GRID_W:]).astype(BF16)
        return carry

    lax.fori_loop(0, rows // NA_ROWS_PER_STEP, body, 0)


def _na_bias_kernel(rpb_ref, o_ref):
    hp = pl.program_id(0)
    shape = (GRID_W, 2 * GRID_W)
    qi = lax.broadcasted_iota(jnp.int32, shape, 0)
    lane = lax.broadcasted_iota(jnp.int32, shape, 1)
    kk = lane & (GRID_W - 1)
    upper = lane >= GRID_W
    coff = jnp.clip(kk - qi, -(NA_KW - 1), NA_KW - 1) + NA_KW - 1
    c0 = jnp.clip(qi - NA_KW // 2, 0, GRID_W - NA_KW)
    valid = (kk >= c0) & (kk < c0 + NA_KW)
    n_ri, n_ci = 2 * NA_KH - 1, 2 * NA_KW - 1
    for hh in range(2):
        head = 2 * hp + hh

        def p_body(p, carry):
            def j_body(j, accs):
                out = []
                for ip in range(NA_KH // 2):
                    base = (head * n_ri + 2 * ip - p + NA_KH - 1) * n_ci + j
                    val = jnp.where(upper, rpb_ref[base + n_ci], rpb_ref[base])
                    out.append(jnp.where(coff == j, val, accs[ip]))
                return tuple(out)

            accs = lax.fori_loop(0, n_ci, j_body, (jnp.zeros(shape, F32),) * (NA_KH // 2))
            for ip in range(NA_KH // 2):
                o_ref[p, hh * GRID_W:(hh + 1) * GRID_W, ip * 128:(ip + 1) * 128] = jnp.where(
                    valid, accs[ip], MASK_VALUE)
            return carry

        lax.fori_loop(0, NA_KH, p_body, 0)


def _na_bias_table(rpb):
    hp = NA_HEADS // 2
    return pl.pallas_call(
        _na_bias_kernel,
        grid=(hp,),
        in_specs=[pl.BlockSpec(memory_space=pltpu.SMEM)],
        out_specs=pl.BlockSpec((None, NA_KH, 2 * GRID_W, NA_KH * GRID_W), lambda h: (h, 0, 0, 0)),
        out_shape=jax.ShapeDtypeStruct((hp, NA_KH, 2 * GRID_W, NA_KH * GRID_W), F32),
        compiler_params=_cparams("parallel"),
        name="na_bias_table",
    )(rpb.astype(F32).reshape(-1))


def _na_attention(proj3, bias, b, s):
    rows = s // GRID_W
    assert rows >= NA_KH and rows % NA_ROWS_PER_STEP == 0
    hp = NA_HEADS // 2
    q0 = 2 * D_FOURIER // 128
    blk = lambda off: pl.BlockSpec((None, s, 128), lambda h, i: (i, 0, off + h))
    return pl.pallas_call(
        functools.partial(_na_kernel, rows=rows),
        grid=(hp, b),
        in_specs=[
            blk(q0), blk(q0 + hp), blk(q0 + 2 * hp),
            pl.BlockSpec((None, NA_KH, 2 * GRID_W, NA_KH * GRID_W), lambda h, i: (h, 0, 0, 0)),
        ],
        out_specs=pl.BlockSpec((None, s, 128), lambda h, i: (i, 0, h)),
        out_shape=jax.ShapeDtypeStruct((b, s, D_NA), BF16),
        compiler_params=_cparams("parallel", "parallel"),
        name="na_attention",
    )(proj3, proj3, proj3, bias)


def _outproj_kernel(yf_ref, yn_ref, x_ref, ge_ref, be_ref, gna_ref, w_ref, g_ref, b_ref, wr_ref,
                    x1_ref, x1b_ref, aff_ref):
    yn = _rms_norm(yn_ref[...].astype(F32), gna_ref[...]).astype(BF16)
    mix = jnp.dot(yf_ref[...], w_ref[:D_FOURIER, :], preferred_element_type=F32)
    mix += jnp.dot(yn, w_ref[D_FOURIER:, :], preferred_element_type=F32)
    x0 = _layer_norm(x_ref[...], ge_ref[...], be_ref[...])
    x1 = _layer_norm(ALPHA * x0 + mix, g_ref[...], b_ref[...])
    x1_ref[...] = x1
    x1b = x1.astype(BF16)
    x1b_ref[...] = x1b
    logits = jnp.dot(x1b, wr_ref[...], preferred_element_type=F32)
    lane = lax.broadcasted_iota(jnp.int32, logits.shape, 1)
    logits = jnp.where(lane < N_EXPERTS, logits, MASK_VALUE)
    m = jnp.max(logits, axis=-1, keepdims=True)
    e = jnp.exp(logits - m)
    aff_ref[...] = e / jnp.sum(e, axis=-1, keepdims=True)


def _outproj(yf, yn, x, ln_emb_g, ln_emb_b, g_na, w_out, ln_g, ln_b, w_router):
    n = x.shape[0]
    tm = min(512, n)
    row = lambda w: pl.BlockSpec((tm, w), lambda i: (i, 0))
    return pl.pallas_call(
        _outproj_kernel,
        grid=(n // tm,),
        in_specs=[row(D_FOURIER), row(D_NA), row(D_MODEL), _resident((1, D_MODEL)),
                  _resident((1, D_MODEL)), _resident((1, D_NA)), _resident((D_MODEL, D_MODEL)),
                  _resident((1, D_MODEL)), _resident((1, D_MODEL)), _resident((D_MODEL, 128))],
        out_specs=[row(D_MODEL), row(D_MODEL), row(128)],
        out_shape=[jax.ShapeDtypeStruct((n, D_MODEL), F32),
                   jax.ShapeDtypeStruct((n, D_MODEL), BF16),
                   jax.ShapeDtypeStruct((n, 128), F32)],
        compiler_params=_cparams("parallel"),
        name="outproj_ln_router",
    )(yf, yn, x, ln_emb_g, ln_emb_b, g_na, w_out, ln_g, ln_b, w_router)


def _expert_up_kernel(x_ref, wg_ref, wu_ref, h_ref, wgb_ref, wub_ref):
    @pl.when(pl.program_id(2) == 0)
    def _():
        wgb_ref[...] = wg_ref[...].astype(BF16)
        wub_ref[...] = wu_ref[...].astype(BF16)

    x = x_ref[...]
    g = jnp.dot(x, wgb_ref[...], preferred_element_type=F32)
    u = jnp.dot(x, wub_ref[...], preferred_element_type=F32)
    h_ref[...] = (g * jax.nn.sigmoid(g) * u).astype(BF16)


def _expert_down_kernel(h_ref, gate_ref, wd_ref, o_ref, wdb_ref):
    @pl.when(pl.program_id(2) == 0)
    def _():
        wdb_ref[...] = wd_ref[...].astype(BF16)

    o_ref[...] = jnp.dot(h_ref[...], wdb_ref[...], preferred_element_type=F32) * gate_ref[...]


def _experts(xe, gate, w_gate, w_up, w_down):
    e, cap, d = xe.shape
    ff = w_gate.shape[-1]
    tm = min(1024, cap)
    tf = 512
    h = pl.pallas_call(
        _expert_up_kernel,
        grid=(e, ff // tf, cap // tm),
        in_specs=[
            pl.BlockSpec((None, tm, d), lambda i, f, j: (i, j, 0)),
            pl.BlockSpec((None, d, tf), lambda i, f, j: (i, 0, f)),
            pl.BlockSpec((None, d, tf), lambda i, f, j: (i, 0, f)),
        ],
        out_specs=pl.BlockSpec((None, tm, tf), lambda i, f, j: (i, j, f)),
        out_shape=jax.ShapeDtypeStruct((e, cap, ff), BF16),
        scratch_shapes=[pltpu.VMEM((d, tf), BF16), pltpu.VMEM((d, tf), BF16)],
        compiler_params=_cparams("parallel", "arbitrary", "arbitrary"),
        name="expert_up",
    )(xe, w_gate, w_up)
    return pl.pallas_call(
        _expert_down_kernel,
        grid=(e, d // tf, cap // tm),
        in_specs=[
            pl.BlockSpec((None, tm, ff), lambda i, f, j: (i, j, 0)),
            pl.BlockSpec((None, tm, 1), lambda i, f, j: (i, j, 0)),
            pl.BlockSpec((None, ff, tf), lambda i, f, j: (i, 0, f)),
        ],
        out_specs=pl.BlockSpec((None, tm, tf), lambda i, f, j: (i, j, f)),
        out_shape=jax.ShapeDtypeStruct((e, cap, d), F32),
        scratch_shapes=[pltpu.VMEM((ff, tf), BF16)],
        compiler_params=_cparams("parallel", "arbitrary", "arbitrary"),
        name="expert_down",
    )(h, gate, w_down)


def _final_kernel(x1_ref, x1b_ref, moe_ref, p_ref, wpp_ref, wpg_ref, g_ref, b_ref, o_ref):
    pp = jnp.dot(p_ref[...].astype(BF16), wpp_ref[...], preferred_element_type=F32)
    pg = jnp.dot(x1b_ref[...], wpg_ref[...], preferred_element_type=F32)
    y = ALPHA * x1_ref[...] + moe_ref[...] + pp * jax.nn.sigmoid(pg)
    o_ref[...] = _layer_norm(y, g_ref[...], b_ref[...])


def _final(x1, x1b, moe, p, w_pp, w_pg, ln_g, ln_b):
    n = x1.shape[0]
    tm = min(512, n)
    row = lambda w: pl.BlockSpec((tm, w), lambda i: (i, 0))
    return pl.pallas_call(
        _final_kernel,
        grid=(n // tm,),
        in_specs=[row(D_MODEL), row(D_MODEL), row(D_MODEL), row(PLE_DIM),
                  _resident((PLE_DIM, D_MODEL)), _resident((D_MODEL, D_MODEL)),
                  _resident((1, D_MODEL)), _resident((1, D_MODEL))],
        out_specs=row(D_MODEL),
        out_shape=jax.ShapeDtypeStruct((n, D_MODEL), F32),
        compiler_params=_cparams("parallel"),
        name="ple_final_ln",
    )(x1, x1b, moe, p, w_pp, w_pg, ln_g, ln_b)


def _trunk(x, p, wts):
    b, s, _ = x.shape
    n = b * s
    x2 = x.reshape(n, D_MODEL)
    proj = _ln_inproj(x2, wts["ln_emb_g"], wts["ln_emb_b"], wts["w_proj"])
    yf = _fourier_branch(proj, wts["g_fourier"], b, s)
    yn = _na_attention(proj.reshape(b, s, D_PROJ), wts["na_bias"], b, s).reshape(n, D_NA)
    x1, x1b, aff = _outproj(yf, yn, x2, wts["ln_emb_g"], wts["ln_emb_b"], wts["g_na"], wts["w_out"],
                            wts["ln1_g"], wts["ln1_b"], wts["w_router"])
    cap = EC_CAPACITY_FACTOR * n // N_EXPERTS
    gate, idx = lax.top_k(aff[:, :N_EXPERTS].T, cap)
    xe = x1b[idx]
    ye = _experts(xe, gate[..., None], wts["w_gate"], wts["w_up"], wts["w_down"])
    moe = jnp.zeros((n, D_MODEL), F32).at[idx.reshape(-1)].add(ye.reshape(-1, D_MODEL))
    y = _final(x1, x1b, moe, p.reshape(n, PLE_DIM), wts["w_ple_proj"], wts["w_ple_gate"],
               wts["ln2_g"], wts["ln2_b"])
    return y.reshape(b, s, D_MODEL)


def _prepare_weights(ln_emb_g, ln_emb_b, w_in, w_fourier, rpb, g_fourier, g_na, w_out, ln1_g, ln1_b,
                     w_router, w_gate, w_up, w_down, w_ple_proj, w_ple_gate, ln2_g, ln2_b):
    assert w_in.shape[0] == 1, "one encoder layer"
    w_zr, w_zi = _fold_fourier(w_in[0], w_fourier[0])
    row = lambda a: a.reshape(1, -1).astype(F32)
    return {
        "ln_emb_g": row(ln_emb_g), "ln_emb_b": row(ln_emb_b),
        "w_proj": jnp.concatenate([w_zr, w_zi, w_in[0][:, D_FOURIER:].astype(BF16)], axis=1),
        "na_bias": _na_bias_table(rpb[0]),
        "g_fourier": row(g_fourier[0]), "g_na": row(g_na[0]),
        "w_out": w_out[0].astype(BF16),
        "ln1_g": row(ln1_g[0]), "ln1_b": row(ln1_b[0]),
        "w_router": jnp.pad(w_router[0], ((0, 0), (0, 128 - N_EXPERTS))).astype(BF16),
        "w_gate": w_gate[0], "w_up": w_up[0], "w_down": w_down[0],
        "w_ple_proj": w_ple_proj[0].astype(BF16), "w_ple_gate": w_ple_gate[0].astype(BF16),
        "ln2_g": row(ln2_g[0]), "ln2_b": row(ln2_b[0]),
    }


def kernel(x_prompt, x_sample, p_prompt, p_sample, ln_emb_g, ln_emb_b, w_in, w_fourier, rpb,
           g_fourier, g_na, w_out, ln1_g, ln1_b, w_router, w_gate, w_up, w_down,
           w_ple_proj, w_ple_gate, ln2_g, ln2_b):
    wts = _prepare_weights(ln_emb_g, ln_emb_b, w_in, w_fourier, rpb, g_fourier, g_na, w_out, ln1_g,
                           ln1_b, w_router, w_gate, w_up, w_down, w_ple_proj, w_ple_gate, ln2_g, ln2_b)
    return (_trunk(x_prompt, p_prompt[0], wts), _trunk(x_sample, p_sample[0], wts))
```

```python
import functools

import numpy as np
import jax
import jax.numpy as jnp
from jax import lax
from jax.experimental import pallas as pl
from jax.experimental.pallas import tpu as pltpu

D_MODEL = 2048
GRID_W = 64
D_FOURIER = 1024
FOURIER_GROUPS = 4
FOURIER_GW = 256
D_NA = 1024
NA_HEADS = 16
NA_HEAD_DIM = 64
NA_KH = 8
NA_KW = 16
N_EXPERTS = 16
EC_CAPACITY_FACTOR = 2
PLE_DIM = 256
LN_EPS = 1e-5
RMS_EPS = 1e-6
ALPHA = float(2.0 ** 0.25)
D_PROJ = 2 * D_FOURIER + 3 * D_NA
MASK_VALUE = -1e30
VMEM_LIMIT = 56 * 1024 * 1024
FFT_S1 = 128

BF16 = jnp.bfloat16
F32 = jnp.float32


def _cparams(*sem):
    return pltpu.CompilerParams(dimension_semantics=sem, vmem_limit_bytes=VMEM_LIMIT)


def _resident(shape):
    return pl.BlockSpec(shape, lambda *_: (0,) * len(shape), pipeline_mode=pl.Buffered(1))


def _layer_norm(x, g, b):
    mu = jnp.mean(x, axis=-1, keepdims=True)
    xc = x - mu
    var = jnp.mean(xc * xc, axis=-1, keepdims=True)
    return xc * lax.rsqrt(var + LN_EPS) * g + b


def _rms_norm(x, g):
    return x * lax.rsqrt(jnp.mean(x * x, axis=-1, keepdims=True) + RMS_EPS) * g


def _fold_kernel(win_ref, wf_ref, cc_ref, sc_ref, zr_ref, zi_ref):
    hi = lax.Precision.HIGHEST
    wf = wf_ref[...]
    a = jnp.dot(cc_ref[...], wf, precision=hi, preferred_element_type=F32)
    b = jnp.dot(sc_ref[...], wf, precision=hi, preferred_element_type=F32)
    w = win_ref[...]
    zr_ref[...] = jnp.dot(w, a, precision=hi, preferred_element_type=F32).astype(BF16)
    zi_ref[...] = (-jnp.dot(w, b, precision=hi, preferred_element_type=F32)).astype(BF16)


def _fold_fourier(w_in, w_fourier):
    c = np.arange(FOURIER_GW)
    ang = 2.0 * np.pi * ((c[:, None] * c[None, :]) % FOURIER_GW) / FOURIER_GW
    cc = jnp.asarray(np.cos(ang), F32)
    sc = jnp.asarray(np.sin(ang), F32)
    gw = FOURIER_GW
    return pl.pallas_call(
        _fold_kernel,
        grid=(FOURIER_GROUPS,),
        in_specs=[
            pl.BlockSpec((D_MODEL, gw), lambda g: (0, g)),
            pl.BlockSpec((None, gw, gw), lambda g: (g, 0, 0)),
            pl.BlockSpec((gw, gw), lambda g: (0, 0)),
            pl.BlockSpec((gw, gw), lambda g: (0, 0)),
        ],
        out_specs=[
            pl.BlockSpec((D_MODEL, gw), lambda g: (0, g)),
            pl.BlockSpec((D_MODEL, gw), lambda g: (0, g)),
        ],
        out_shape=[jax.ShapeDtypeStruct((D_MODEL, D_FOURIER), BF16)] * 2,
        compiler_params=_cparams("arbitrary"),
        name="fold_fourier",
    )(w_in, w_fourier, cc, sc)


def _ln_inproj_kernel(x_ref, g_ref, b_ref, w_ref, proj_ref, xn_ref):
    @pl.when(pl.program_id(1) == 0)
    def _():
        xn_ref[...] = _layer_norm(x_ref[...], g_ref[...], b_ref[...]).astype(BF16)

    proj_ref[...] = jnp.dot(xn_ref[...], w_ref[...], preferred_element_type=F32).astype(BF16)


def _ln_inproj(x, g, b, w):
    n = x.shape[0]
    tm = min(1024, n)
    tn = 1024
    return pl.pallas_call(
        _ln_inproj_kernel,
        grid=(n // tm, D_PROJ // tn),
        in_specs=[
            pl.BlockSpec((tm, D_MODEL), lambda i, j: (i, 0)),
            pl.BlockSpec((1, D_MODEL), lambda i, j: (0, 0)),
            pl.BlockSpec((1, D_MODEL), lambda i, j: (0, 0)),
            pl.BlockSpec((D_MODEL, tn), lambda i, j: (0, j)),
        ],
        out_specs=pl.BlockSpec((tm, tn), lambda i, j: (i, j)),
        out_shape=jax.ShapeDtypeStruct((n, D_PROJ), BF16),
        scratch_shapes=[pltpu.VMEM((tm, D_MODEL), BF16)],
        compiler_params=_cparams("parallel", "arbitrary"),
        name="ln_inproj",
    )(x, g, b, w)


def _fft1_kernel(z_ref, m_ref, y_ref, *, tb):
    for t in range(tb):
        x = jnp.concatenate([z_ref[t, :, :D_FOURIER], z_ref[t, :, D_FOURIER:]], axis=0)
        y = jnp.dot(m_ref[t], x, preferred_element_type=F32)
        s1 = y.shape[0] // 2
        y_ref[t, :, :D_FOURIER] = y[:s1].astype(BF16)
        y_ref[t, :, D_FOURIER:] = y[s1:].astype(BF16)


def _fft1_matrices(s1, s2):
    s = s1 * s2
    k1 = np.arange(s1)[None, :, None]
    n = (s2 * np.arange(s1)[None, None, :] + np.arange(s2)[:, None, None])
    ang = 2.0 * np.pi * ((k1 * n) % s) / s
    fr, fi = np.cos(ang), -np.sin(ang)
    m = np.concatenate([np.concatenate([fr, -fi], axis=2),
                        np.concatenate([fi, fr], axis=2)], axis=1)
    return jnp.asarray(m, F32).astype(BF16)


def _fft1(zt, s1, s2):
    b = zt.shape[0]
    tb = min(4, s2)
    m = _fft1_matrices(s1, s2)
    return pl.pallas_call(
        functools.partial(_fft1_kernel, tb=tb),
        grid=(s2 // tb, b),
        in_specs=[
            pl.BlockSpec((None, tb, s1, 2 * D_FOURIER), lambda j, i: (i, j, 0, 0)),
            pl.BlockSpec((tb, 2 * s1, 2 * s1), lambda j, i: (j, 0, 0)),
        ],
        out_specs=pl.BlockSpec((None, tb, s1, 2 * D_FOURIER), lambda j, i: (i, j, 0, 0)),
        out_shape=jax.ShapeDtypeStruct(zt.shape, BF16),
        compiler_params=_cparams("parallel", "parallel"),
        name="fft_stage1",
    )(zt, m)


def _fft2_kernel(v_ref, m_ref, g_ref, o_ref, *, tb):
    m = m_ref[...]
    g = g_ref[...]
    for t in range(tb):
        x = jnp.concatenate([v_ref[t, :, :D_FOURIER], v_ref[t, :, D_FOURIER:]], axis=0)
        y = jnp.dot(m, x, preferred_element_type=F32)
        o_ref[t] = _rms_norm(y, g).astype(BF16)


def _fft2(v, g_fourier, s1, s2):
    b = v.shape[0]
    tb = 8
    k2 = np.arange(s2)
    ang = 2.0 * np.pi * ((k2[:, None] * k2[None, :]) % s2) / s2
    m = jnp.asarray(np.concatenate([np.cos(ang), np.sin(ang)], axis=1), F32).astype(BF16)
    return pl.pallas_call(
        functools.partial(_fft2_kernel, tb=tb),
        grid=(b, s1 // tb),
        in_specs=[
            pl.BlockSpec((None, tb, s2, 2 * D_FOURIER), lambda i, j: (i, j, 0, 0)),
            pl.BlockSpec((s2, 2 * s2), lambda i, j: (0, 0)),
            pl.BlockSpec((1, D_FOURIER), lambda i, j: (0, 0)),
        ],
        out_specs=pl.BlockSpec((None, tb, s2, D_FOURIER), lambda i, j: (i, j, 0, 0)),
        out_shape=jax.ShapeDtypeStruct((b, s1, s2, D_FOURIER), BF16),
        compiler_params=_cparams("parallel", "parallel"),
        name="fft_stage2",
    )(v, m, g_fourier)


def _fourier_branch(proj, g_fourier, b, s):
    s1, s2 = FFT_S1, s // FFT_S1
    z = proj[:, :2 * D_FOURIER].reshape(b, s1, s2, 2 * D_FOURIER)
    y = _fft1(jnp.transpose(z, (0, 2, 1, 3)), s1, s2)
    o = _fft2(jnp.transpose(y, (0, 2, 1, 3)), g_fourier, s1, s2)
    return jnp.transpose(o, (0, 2, 1, 3)).reshape(b * s, D_FOURIER)


NA_ROWS_PER_STEP = 4


def _na_kernel(q_ref, k_ref, v_ref, bias_ref, o_ref, *, rows):
    nk = NA_KH * GRID_W
    lane = lax.broadcasted_iota(jnp.int32, (GRID_W, 2 * NA_HEAD_DIM), 1)
    first = lane < NA_HEAD_DIM
    scale = NA_HEAD_DIM ** -0.5
    head0 = (first.astype(F32) * scale).astype(BF16)
    head1 = ((1.0 - first.astype(F32)) * scale).astype(BF16)

    def body(i, carry):
        rr = [i * NA_ROWS_PER_STEP + u for u in range(NA_ROWS_PER_STEP)]
        rs = [jnp.clip(r - NA_KH // 2, 0, rows - NA_KH) for r in rr]
        tok = lambda r: pl.ds(pl.multiple_of(r * GRID_W, GRID_W), GRID_W)
        win = lambda r: pl.ds(pl.multiple_of(r * GRID_W, GRID_W), nk)
        scores = []
        for r, r0 in zip(rr, rs):
            q = q_ref[tok(r), :]
            qbd = jnp.concatenate([q * head0, q * head1], axis=0)
            s = lax.dot_general(qbd, k_ref[win(r0), :], (((1,), (1,)), ((), ())),
                                preferred_element_type=F32)
            scores.append(s + bias_ref[r - r0])
        probs = []
        for s in scores:
            p = jnp.exp(s - jnp.max(s, axis=-1, keepdims=True))
            probs.append((p.astype(BF16), jnp.sum(p, axis=-1, keepdims=True)))
        for r, r0, (p, l) in zip(rr, rs, probs):
            o = jnp.dot(p, v_ref[win(r0), :], preferred_element_type=F32) / l
            o_ref[tok(r), :] = jnp.where(first, o[:GRID_W], o[GRID_W:]).astype(BF16)
        return carry

    lax.fori_loop(0, rows // NA_ROWS_PER_STEP, body, 0)


def _na_bias_kernel(rpb_ref, o_ref):
    hp = pl.program_id(0)
    shape = (GRID_W, 2 * GRID_W)
    qi = lax.broadcasted_iota(jnp.int32, shape, 0)
    lane = lax.broadcasted_iota(jnp.int32, shape, 1)
    kk = lane & (GRID_W - 1)
    upper = lane >= GRID_W
    coff = jnp.clip(kk - qi, -(NA_KW - 1), NA_KW - 1) + NA_KW - 1
    c0 = jnp.clip(qi - NA_KW // 2, 0, GRID_W - NA_KW)
    valid = (kk >= c0) & (kk < c0 + NA_KW)
    n_ri, n_ci = 2 * NA_KH - 1, 2 * NA_KW - 1
    for hh in range(2):
        head = 2 * hp + hh

        def p_body(p, carry):
            def j_body(j, accs):
                out = []
                for ip in range(NA_KH // 2):
                    base = (head * n_ri + 2 * ip - p + NA_KH - 1) * n_ci + j
                    val = jnp.where(upper, rpb_ref[base + n_ci], rpb_ref[base])
                    out.append(jnp.where(coff == j, val, accs[ip]))
                return tuple(out)

            accs = lax.fori_loop(0, n_ci, j_body, (jnp.zeros(shape, F32),) * (NA_KH // 2))
            for ip in range(NA_KH // 2):
                o_ref[p, hh * GRID_W:(hh + 1) * GRID_W, ip * 128:(ip + 1) * 128] = jnp.where(
                    valid, accs[ip], MASK_VALUE)
            return carry

        lax.fori_loop(0, NA_KH, p_body, 0)


def _na_bias_table(rpb):
    hp = NA_HEADS // 2
    return pl.pallas_call(
        _na_bias_kernel,
        grid=(hp,),
        in_specs=[pl.BlockSpec(memory_space=pltpu.SMEM)],
        out_specs=pl.BlockSpec((None, NA_KH, 2 * GRID_W, NA_KH * GRID_W), lambda h: (h, 0, 0, 0)),
        out_shape=jax.ShapeDtypeStruct((hp, NA_KH, 2 * GRID_W, NA_KH * GRID_W), F32),
        compiler_params=_cparams("parallel"),
        name="na_bias_table",
    )(rpb.astype(F32).reshape(-1))


def _na_attention(proj3, bias, b, s):
    rows = s // GRID_W
    assert rows >= NA_KH and rows % NA_ROWS_PER_STEP == 0
    hp = NA_HEADS // 2
    q0 = 2 * D_FOURIER // 128
    blk = lambda off: pl.BlockSpec((None, s, 128), lambda h, i: (i, 0, off + h))
    return pl.pallas_call(
        functools.partial(_na_kernel, rows=rows),
        grid=(hp, b),
        in_specs=[
            blk(q0), blk(q0 + hp), blk(q0 + 2 * hp),
            pl.BlockSpec((None, NA_KH, 2 * GRID_W, NA_KH * GRID_W), lambda h, i: (h, 0, 0, 0)),
        ],
        out_specs=pl.BlockSpec((None, s, 128), lambda h, i: (i, 0, h)),
        out_shape=jax.ShapeDtypeStruct((b, s, D_NA), BF16),
        compiler_params=_cparams("parallel", "parallel"),
        name="na_attention",
    )(proj3, proj3, proj3, bias)


def _outproj_kernel(yf_ref, yn_ref, x_ref, ge_ref, be_ref, gna_ref, w_ref, g_ref, b_ref, wr_ref,
                    x1_ref, x1b_ref, aff_ref):
    yn = _rms_norm(yn_ref[...].astype(F32), gna_ref[...]).astype(BF16)
    mix = jnp.dot(yf_ref[...], w_ref[:D_FOURIER, :], preferred_element_type=F32)
    mix += jnp.dot(yn, w_ref[D_FOURIER:, :], preferred_element_type=F32)
    x0 = _layer_norm(x_ref[...], ge_ref[...], be_ref[...])
    x1 = _layer_norm(ALPHA * x0 + mix, g_ref[...], b_ref[...])
    x1_ref[...] = x1
    x1b = x1.astype(BF16)
    x1b_ref[...] = x1b
    logits = jnp.dot(x1b, wr_ref[...], preferred_element_type=F32)
    lane = lax.broadcasted_iota(jnp.int32, logits.shape, 1)
    logits = jnp.where(lane < N_EXPERTS, logits, MASK_VALUE)
    m = jnp.max(logits, axis=-1, keepdims=True)
    e = jnp.exp(logits - m)
    aff_ref[...] = e / jnp.sum(e, axis=-1, keepdims=True)


def _outproj(yf, yn, x, ln_emb_g, ln_emb_b, g_na, w_out, ln_g, ln_b, w_router):
    n = x.shape[0]
    tm = min(512, n)
    row = lambda w: pl.BlockSpec((tm, w), lambda i: (i, 0))
    return pl.pallas_call(
        _outproj_kernel,
        grid=(n // tm,),
        in_specs=[row(D_FOURIER), row(D_NA), row(D_MODEL), _resident((1, D_MODEL)),
                  _resident((1, D_MODEL)), _resident((1, D_NA)), _resident((D_MODEL, D_MODEL)),
                  _resident((1, D_MODEL)), _resident((1, D_MODEL)), _resident((D_MODEL, 128))],
        out_specs=[row(D_MODEL), row(D_MODEL), row(128)],
        out_shape=[jax.ShapeDtypeStruct((n, D_MODEL), F32),
                   jax.ShapeDtypeStruct((n, D_MODEL), BF16),
                   jax.ShapeDtypeStruct((n, 128), F32)],
        compiler_params=_cparams("parallel"),
        name="outproj_ln_router",
    )(yf, yn, x, ln_emb_g, ln_emb_b, g_na, w_out, ln_g, ln_b, w_router)


def _tri(n, strict, lower):
    r = lax.broadcasted_iota(jnp.int32, (n, n), 0)
    c = lax.broadcasted_iota(jnp.int32, (n, n), 1)
    keep = (c < r if strict else c <= r) if lower else (r < c if strict else r <= c)
    return keep.astype(F32).astype(BF16)


def _prefix_counts(m3):
    e, r, _ = m3.shape
    cin = jnp.dot(m3.reshape(e * r, 128).astype(BF16), _tri(128, False, False),
                  preferred_element_type=F32).reshape(e, r, 128)
    tot = jnp.broadcast_to(cin[:, :, 127:128], (e, r, 128)).astype(BF16)
    low = _tri(r, True, True)
    roff = jnp.stack([jnp.dot(low, tot[i], preferred_element_type=F32) for i in range(e)])
    return cin, roff


def _count(m3):
    return jnp.sum(jnp.sum(m3, axis=1, keepdims=True), axis=2, keepdims=True)


def _select_kernel(aff_ref, sel_ref, slot_ref, *, cap):
    aff = aff_ref[...]
    e = aff.shape[0]

    def bisect(it, thr_bits):
        cand = thr_bits | (jnp.int32(1) << (30 - it))
        n_ge = _count((aff >= pltpu.bitcast(cand, F32)).astype(F32))
        return jnp.where(n_ge >= cap, cand, thr_bits)

    thr = pltpu.bitcast(lax.fori_loop(0, 31, bisect, jnp.zeros((e, 1, 1), jnp.int32)), F32)
    gt = (aff > thr).astype(F32)
    eq = (aff == thr).astype(F32)
    need = cap - _count(gt)
    cin, roff = _prefix_counts(eq)
    eq_rank = roff + cin - eq
    sel = jnp.maximum(gt, eq * (eq_rank < need).astype(F32))
    cin, roff = _prefix_counts(sel)
    sel_ref[...] = sel
    slot_ref[...] = (roff + cin - sel).astype(jnp.int32)


def _select(aff3, cap):
    e, r, _ = aff3.shape
    return pl.pallas_call(
        functools.partial(_select_kernel, cap=cap),
        out_shape=[jax.ShapeDtypeStruct((e, r, 128), F32),
                   jax.ShapeDtypeStruct((e, r, 128), jnp.int32)],
        compiler_params=pltpu.CompilerParams(vmem_limit_bytes=VMEM_LIMIT),
        name="expert_select",
    )(aff3)


def _index_kernel(sel_ref, idx_ref, *, cap):
    m = sel_ref[...]
    r = m.shape[0]
    mb = m.astype(BF16)
    cin = jnp.dot(mb, _tri(128, False, False), preferred_element_type=F32)
    ones = jnp.ones((8, 128), BF16)
    tot_row = lax.dot_general(ones, mb, (((1,), (1,)), ((), ())),
                              preferred_element_type=F32)
    end_row = jnp.dot(tot_row.astype(BF16), _tri(r, False, False),
                      preferred_element_type=F32)[0:1]
    off_row = end_row - tot_row[0:1]
    j = lax.broadcasted_iota(jnp.int32, (cap, r), 0).astype(F32)
    onehot = jnp.logical_and(j >= off_row, j < end_row).astype(F32).astype(BF16)
    row_id = lax.broadcasted_iota(jnp.int32, (r, 128), 0)
    tot = jnp.broadcast_to(cin[:, 127:128], (r, 128)).astype(BF16)
    roff = jnp.dot(_tri(r, True, True), tot, preferred_element_type=F32)
    pick = lambda mat: jnp.dot(onehot, mat.astype(BF16), preferred_element_type=F32)
    cin_j = pick(cin)
    row_j = pick(row_id.astype(F32))
    roff_hi = jnp.floor(roff * (1.0 / 128.0))
    roff_j = pick(roff_hi) * 128.0 + pick(roff - roff_hi * 128.0)
    jloc = lax.broadcasted_iota(jnp.int32, (cap, 128), 0).astype(F32) - roff_j
    lane_j = jnp.dot((cin_j <= jloc).astype(F32).astype(BF16), jnp.ones((128, 128), BF16),
                     preferred_element_type=F32)
    idx_ref[...] = (row_j * 128.0 + lane_j).astype(jnp.int32)


def _slot_index(sel3, cap):
    e, r, _ = sel3.shape
    return pl.pallas_call(
        functools.partial(_index_kernel, cap=cap),
        grid=(e,),
        in_specs=[pl.BlockSpec((None, r, 128), lambda i: (i, 0, 0))],
        out_specs=pl.BlockSpec((None, cap, 128), lambda i: (i, 0, 0)),
        out_shape=jax.ShapeDtypeStruct((e, cap, 128), jnp.int32),
        compiler_params=_cparams("parallel"),
        name="expert_slot_index",
    )(sel3)


def _expert_up_kernel(x_ref, wg_ref, wu_ref, h_ref, wgb_ref, wub_ref):
    @pl.when(pl.program_id(2) == 0)
    def _():
        wgb_ref[...] = wg_ref[...].astype(BF16)
        wub_ref[...] = wu_ref[...].astype(BF16)

    x = x_ref[...]
    g = jnp.dot(x, wgb_ref[...], preferred_element_type=F32)
    u = jnp.dot(x, wub_ref[...], preferred_element_type=F32)
    h_ref[...] = (g * jax.nn.sigmoid(g) * u).astype(BF16)


def _expert_down_kernel(h_ref, gate_ref, wd_ref, o_ref, wdb_ref):
    @pl.when(pl.program_id(2) == 0)
    def _():
        wdb_ref[...] = wd_ref[...].astype(BF16)

    y = jnp.dot(h_ref[...], wdb_ref[...], preferred_element_type=F32) * gate_ref[...]
    o_ref[...] = y.astype(BF16)


def _experts(xe, gate, w_gate, w_up, w_down):
    e, cap, d = xe.shape
    ff = w_gate.shape[-1]
    tm = min(1024, cap)
    tf = 512
    tn = 1024
    h = pl.pallas_call(
        _expert_up_kernel,
        grid=(e, ff // tf, cap // tm),
        in_specs=[
            pl.BlockSpec((None, tm, d), lambda i, f, j: (i, j, 0)),
            pl.BlockSpec((None, d, tf), lambda i, f, j: (i, 0, f)),
            pl.BlockSpec((None, d, tf), lambda i, f, j: (i, 0, f)),
        ],
        out_specs=pl.BlockSpec((None, tm, tf), lambda i, f, j: (i, j, f)),
        out_shape=jax.ShapeDtypeStruct((e, cap, ff), BF16),
        scratch_shapes=[pltpu.VMEM((d, tf), BF16), pltpu.VMEM((d, tf), BF16)],
        compiler_params=_cparams("parallel", "arbitrary", "arbitrary"),
        name="expert_up",
    )(xe, w_gate, w_up)
    return pl.pallas_call(
        _expert_down_kernel,
        grid=(e, d // tn, cap // tm),
        in_specs=[
            pl.BlockSpec((None, tm, ff), lambda i, f, j: (i, j, 0)),
            pl.BlockSpec((None, tm, 1), lambda i, f, j: (i, j, 0)),
            pl.BlockSpec((None, ff, tn), lambda i, f, j: (i, 0, f)),
        ],
        out_specs=pl.BlockSpec((None, tm, tn), lambda i, f, j: (i, j, f)),
        out_shape=jax.ShapeDtypeStruct((e, cap, d), BF16),
        scratch_shapes=[pltpu.VMEM((ff, tn), BF16)],
        compiler_params=_cparams("parallel", "arbitrary", "arbitrary"),
        name="expert_down",
    )(h, gate, w_down)


COMBINE_TOKENS = 256
COMBINE_WIN = 64
WIN_ALIGN = 16
NOT_SELECTED = -(1 << 30)


def _final_kernel(woff_ref, npass_ref, x1_ref, x1b_ref, slot_ref, p_ref, wpp_ref, wpg_ref, g_ref,
                  b_ref, ye_ref, o_ref, win_ref, sem, *, cap):
    i = pl.program_id(0)
    n_pass = npass_ref[i]
    t = x1_ref.shape[0]

    def window_start(e, q):
        lo = woff_ref[i, e] + q * COMBINE_WIN
        return lo, pl.multiple_of(jnp.minimum(lo, cap - COMBINE_WIN), WIN_ALIGN)

    def window_copy(e, q):
        _, start = window_start(e, q)
        return pltpu.make_async_copy(ye_ref.at[e, pl.ds(start, COMBINE_WIN)],
                                     win_ref.at[pl.ds(e * COMBINE_WIN, COMBINE_WIN)], sem.at[0])

    def start_windows(q):
        for e in range(N_EXPERTS):
            window_copy(e, q).start()

    @pl.when(n_pass > 0)
    def _():
        start_windows(0)

    pp = jnp.dot(p_ref[...].astype(BF16), wpp_ref[...], preferred_element_type=F32)
    pg = jnp.dot(x1b_ref[...], wpg_ref[...], preferred_element_type=F32)
    base = ALPHA * x1_ref[...] + pp * jax.nn.sigmoid(pg)

    slot = slot_ref[...]
    lane = lax.broadcasted_iota(jnp.int32, (t, 2 * COMBINE_WIN), 1)
    upper = lane >= COMBINE_WIN
    col = lane & (COMBINE_WIN - 1)

    def one_pass(q, moe):
        for e in range(N_EXPERTS):
            window_copy(e, q).wait()
        blocks = []
        for pair in range(N_EXPERTS // 2):
            e0, e1 = 2 * pair, 2 * pair + 1
            lo0, st0 = window_start(e0, q)
            lo1, st1 = window_start(e1, q)
            rel = jnp.where(upper, slot[:, e1:e1 + 1] - lo1, slot[:, e0:e0 + 1] - lo0)
            shift = jnp.where(upper, lo1 - st1, lo0 - st0)
            hit = (rel >= 0) & (rel < COMBINE_WIN) & (col == rel + shift)
            blocks.append(hit.astype(F32).astype(BF16))
        route = jnp.concatenate(blocks, axis=1)
        moe = moe + jnp.dot(route, win_ref[...], preferred_element_type=F32)

        @pl.when(q + 1 < n_pass)
        def _():
            start_windows(q + 1)

        return moe

    moe = lax.fori_loop(0, n_pass, one_pass, jnp.zeros(base.shape, F32))
    o_ref[...] = _layer_norm(base + moe, g_ref[...], b_ref[...])


def _final(x1, x1b, slot_tok, woff, npass, ye, p, w_pp, w_pg, ln_g, ln_b):
    n = x1.shape[0]
    cap = ye.shape[1]
    tm = min(COMBINE_TOKENS, n)
    assert cap >= COMBINE_WIN and cap % WIN_ALIGN == 0
    row = lambda w: pl.BlockSpec((tm, w), lambda i, *_: (i, 0))
    return pl.pallas_call(
        functools.partial(_final_kernel, cap=cap),
        grid_spec=pltpu.PrefetchScalarGridSpec(
            num_scalar_prefetch=2,
            grid=(n // tm,),
            in_specs=[row(D_MODEL), row(D_MODEL), row(N_EXPERTS), row(PLE_DIM),
                      _resident((PLE_DIM, D_MODEL)), _resident((D_MODEL, D_MODEL)),
                      _resident((1, D_MODEL)), _resident((1, D_MODEL)),
                      pl.BlockSpec(memory_space=pl.ANY)],
            out_specs=row(D_MODEL),
            scratch_shapes=[pltpu.VMEM((N_EXPERTS * COMBINE_WIN, D_MODEL), BF16),
                            pltpu.SemaphoreType.DMA((1,))],
        ),
        out_shape=jax.ShapeDtypeStruct((n, D_MODEL), F32),
        compiler_params=_cparams("arbitrary"),
        name="ple_combine_final_ln",
    )(woff, npass, x1, x1b, slot_tok, p, w_pp, w_pg, ln_g, ln_b, ye)


def _dispatch_tables(sel3, slot3, n, cap):
    tm = min(COMBINE_TOKENS, n)
    sel = sel3.reshape(N_EXPERTS, n) > 0
    slot = slot3.reshape(N_EXPERTS, n)
    slot_tok = jnp.where(sel, slot, NOT_SELECTED).T
    first = slot[:, ::tm]
    last = jnp.concatenate([first[:, 1:], jnp.full((N_EXPERTS, 1), cap, jnp.int32)], axis=1)
    woff = (first // WIN_ALIGN) * WIN_ALIGN
    passes = jnp.where(last > first, (last - woff + COMBINE_WIN - 1) // COMBINE_WIN, 0)
    return slot_tok, woff.T, jnp.max(passes, axis=0).astype(jnp.int32)


def _trunk(x, p, wts):
    b, s, _ = x.shape
    n = b * s
    x2 = x.reshape(n, D_MODEL)
    proj = _ln_inproj(x2, wts["ln_emb_g"], wts["ln_emb_b"], wts["w_proj"])
    yf = _fourier_branch(proj, wts["g_fourier"], b, s)
    yn = _na_attention(proj.reshape(b, s, D_PROJ), wts["na_bias"], b, s).reshape(n, D_NA)
    x1, x1b, aff = _outproj(yf, yn, x2, wts["ln_emb_g"], wts["ln_emb_b"], wts["g_na"], wts["w_out"],
                            wts["ln1_g"], wts["ln1_b"], wts["w_router"])
    cap = EC_CAPACITY_FACTOR * n // N_EXPERTS
    aff_t = aff[:, :N_EXPERTS].T
    sel3, slot3 = _select(aff_t.reshape(N_EXPERTS, n // 128, 128), cap)
    idx = _slot_index(sel3, cap)[:, :, 0]
    gate = jnp.take_along_axis(aff_t, idx, axis=1)
    xe = x1b[idx]
    ye = _experts(xe, gate[..., None], wts["w_gate"], wts["w_up"], wts["w_down"])
    slot_tok, woff, npass = _dispatch_tables(sel3, slot3, n, cap)
    y = _final(x1, x1b, slot_tok, woff, npass, ye, p.reshape(n, PLE_DIM), wts["w_ple_proj"],
               wts["w_ple_gate"], wts["ln2_g"], wts["ln2_b"])
    return y.reshape(b, s, D_MODEL)


def _prepare_weights(ln_emb_g, ln_emb_b, w_in, w_fourier, rpb, g_fourier, g_na, w_out, ln1_g, ln1_b,
                     w_router, w_gate, w_up, w_down, w_ple_proj, w_ple_gate, ln2_g, ln2_b):
    assert w_in.shape[0] == 1, "one encoder layer"
    w_zr, w_zi = _fold_fourier(w_in[0], w_fourier[0])
    row = lambda a: a.reshape(1, -1).astype(F32)
    return {
        "ln_emb_g": row(ln_emb_g), "ln_emb_b": row(ln_emb_b),
        "w_proj": jnp.concatenate([w_zr, w_zi, w_in[0][:, D_FOURIER:].astype(BF16)], axis=1),
        "na_bias": _na_bias_table(rpb[0]),
        "g_fourier": row(g_fourier[0]), "g_na": row(g_na[0]),
        "w_out": w_out[0].astype(BF16),
        "ln1_g": row(ln1_g[0]), "ln1_b": row(ln1_b[0]),
        "w_router": jnp.pad(w_router[0], ((0, 0), (0, 128 - N_EXPERTS))).astype(BF16),
        "w_gate": w_gate[0], "w_up": w_up[0], "w_down": w_down[0],
        "w_ple_proj": w_ple_proj[0].astype(BF16), "w_ple_gate": w_ple_gate[0].astype(BF16),
        "ln2_g": row(ln2_g[0]), "ln2_b": row(ln2_b[0]),
    }


def kernel(x_prompt, x_sample, p_prompt, p_sample, ln_emb_g, ln_emb_b, w_in, w_fourier, rpb,
           g_fourier, g_na, w_out, ln1_g, ln1_b, w_router, w_gate, w_up, w_down,
           w_ple_proj, w_ple_gate, ln2_g, ln2_b):
    wts = _prepare_weights(ln_emb_g, ln_emb_b, w_in, w_fourier, rpb, g_fourier, g_na, w_out, ln1_g,
                           ln1_b, w_router, w_gate, w_up, w_down, w_ple_proj, w_ple_gate, ln2_g, ln2_b)
    return (_trunk(x_prompt, p_prompt[0], wts), _trunk(x_sample, p_sample[0], wts))
```

```python
import functools

import numpy as np
import jax
import jax.numpy as jnp
from jax import lax
from jax.experimental import pallas as pl
from jax.experimental.pallas import tpu as pltpu

D_MODEL = 2048
GRID_W = 64
D_FOURIER = 1024
FOURIER_GROUPS = 4
FOURIER_GW = 256
D_NA = 1024
NA_HEADS = 16
NA_HEAD_DIM = 64
NA_KH = 8
NA_KW = 16
N_EXPERTS = 16
EC_CAPACITY_FACTOR = 2
PLE_DIM = 256
LN_EPS = 1e-5
RMS_EPS = 1e-6
ALPHA = float(2.0 ** 0.25)
D_PROJ = 2 * D_FOURIER + 3 * D_NA
MASK_VALUE = -1e30
VMEM_LIMIT = 56 * 1024 * 1024
FFT_S1 = 128

BF16 = jnp.bfloat16
F32 = jnp.float32


def _cparams(*sem):
    return pltpu.CompilerParams(dimension_semantics=sem, vmem_limit_bytes=VMEM_LIMIT)


def _resident(shape):
    return pl.BlockSpec(shape, lambda *_: (0,) * len(shape), pipeline_mode=pl.Buffered(1))


def _layer_norm(x, g, b):
    mu = jnp.mean(x, axis=-1, keepdims=True)
    xc = x - mu
    var = jnp.mean(xc * xc, axis=-1, keepdims=True)
    return xc * lax.rsqrt(var + LN_EPS) * g + b


def _rms_norm(x, g):
    return x * lax.rsqrt(jnp.mean(x * x, axis=-1, keepdims=True) + RMS_EPS) * g


def _fold_kernel(win_ref, wf_ref, cc_ref, sc_ref, zr_ref, zi_ref):
    hi = lax.Precision.HIGHEST
    wf = wf_ref[...]
    a = jnp.dot(cc_ref[...], wf, precision=hi, preferred_element_type=F32)
    b = jnp.dot(sc_ref[...], wf, precision=hi, preferred_element_type=F32)
    w = win_ref[...]
    zr_ref[...] = jnp.dot(w, a, precision=hi, preferred_element_type=F32).astype(BF16)
    zi_ref[...] = (-jnp.dot(w, b, precision=hi, preferred_element_type=F32)).astype(BF16)


def _fold_fourier(w_in, w_fourier):
    c = np.arange(FOURIER_GW)
    ang = 2.0 * np.pi * ((c[:, None] * c[None, :]) % FOURIER_GW) / FOURIER_GW
    cc = jnp.asarray(np.cos(ang), F32)
    sc = jnp.asarray(np.sin(ang), F32)
    gw = FOURIER_GW
    return pl.pallas_call(
        _fold_kernel,
        grid=(FOURIER_GROUPS,),
        in_specs=[
            pl.BlockSpec((D_MODEL, gw), lambda g: (0, g)),
            pl.BlockSpec((None, gw, gw), lambda g: (g, 0, 0)),
            pl.BlockSpec((gw, gw), lambda g: (0, 0)),
            pl.BlockSpec((gw, gw), lambda g: (0, 0)),
        ],
        out_specs=[
            pl.BlockSpec((D_MODEL, gw), lambda g: (0, g)),
            pl.BlockSpec((D_MODEL, gw), lambda g: (0, g)),
        ],
        out_shape=[jax.ShapeDtypeStruct((D_MODEL, D_FOURIER), BF16)] * 2,
        compiler_params=_cparams("arbitrary"),
        name="fold_fourier",
    )(w_in, w_fourier, cc, sc)


def _ln_inproj_kernel(x_ref, g_ref, b_ref, w_ref, proj_ref, xn_ref):
    @pl.when(pl.program_id(1) == 0)
    def _():
        xn_ref[...] = _layer_norm(x_ref[...], g_ref[...], b_ref[...]).astype(BF16)

    proj_ref[...] = jnp.dot(xn_ref[...], w_ref[...], preferred_element_type=F32).astype(BF16)


def _ln_inproj(x, g, b, w):
    n = x.shape[0]
    tm = min(1024, n)
    tn = 1024
    return pl.pallas_call(
        _ln_inproj_kernel,
        grid=(n // tm, D_PROJ // tn),
        in_specs=[
            pl.BlockSpec((tm, D_MODEL), lambda i, j: (i, 0)),
            pl.BlockSpec((1, D_MODEL), lambda i, j: (0, 0)),
            pl.BlockSpec((1, D_MODEL), lambda i, j: (0, 0)),
            pl.BlockSpec((D_MODEL, tn), lambda i, j: (0, j)),
        ],
        out_specs=pl.BlockSpec((tm, tn), lambda i, j: (i, j)),
        out_shape=jax.ShapeDtypeStruct((n, D_PROJ), BF16),
        scratch_shapes=[pltpu.VMEM((tm, D_MODEL), BF16)],
        compiler_params=_cparams("parallel", "arbitrary"),
        name="ln_inproj",
    )(x, g, b, w)


def _fft1_kernel(z_ref, m_ref, y_ref, *, tb):
    for t in range(tb):
        x = jnp.concatenate([z_ref[t, :, :D_FOURIER], z_ref[t, :, D_FOURIER:]], axis=0)
        y = jnp.dot(m_ref[t], x, preferred_element_type=F32)
        s1 = y.shape[0] // 2
        y_ref[t, :, :D_FOURIER] = y[:s1].astype(BF16)
        y_ref[t, :, D_FOURIER:] = y[s1:].astype(BF16)


def _fft1_matrices(s1, s2):
    s = s1 * s2
    k1 = np.arange(s1)[None, :, None]
    n = (s2 * np.arange(s1)[None, None, :] + np.arange(s2)[:, None, None])
    ang = 2.0 * np.pi * ((k1 * n) % s) / s
    fr, fi = np.cos(ang), -np.sin(ang)
    m = np.concatenate([np.concatenate([fr, -fi], axis=2),
                        np.concatenate([fi, fr], axis=2)], axis=1)
    return jnp.asarray(m, F32).astype(BF16)


def _fft1(zt, s1, s2):
    b = zt.shape[0]
    tb = min(4, s2)
    m = _fft1_matrices(s1, s2)
    return pl.pallas_call(
        functools.partial(_fft1_kernel, tb=tb),
        grid=(s2 // tb, b),
        in_specs=[
            pl.BlockSpec((None, tb, s1, 2 * D_FOURIER), lambda j, i: (i, j, 0, 0)),
            pl.BlockSpec((tb, 2 * s1, 2 * s1), lambda j, i: (j, 0, 0)),
        ],
        out_specs=pl.BlockSpec((None, tb, s1, 2 * D_FOURIER), lambda j, i: (i, j, 0, 0)),
        out_shape=jax.ShapeDtypeStruct(zt.shape, BF16),
        compiler_params=_cparams("parallel", "parallel"),
        name="fft_stage1",
    )(zt, m)


def _fft2_kernel(v_ref, m_ref, g_ref, o_ref, *, tb):
    m = m_ref[...]
    g = g_ref[...]
    for t in range(tb):
        x = jnp.concatenate([v_ref[t, :, :D_FOURIER], v_ref[t, :, D_FOURIER:]], axis=0)
        y = jnp.dot(m, x, preferred_element_type=F32)
        o_ref[t] = _rms_norm(y, g).astype(BF16)


def _fft2(v, g_fourier, s1, s2):
    b = v.shape[0]
    tb = 8
    k2 = np.arange(s2)
    ang = 2.0 * np.pi * ((k2[:, None] * k2[None, :]) % s2) / s2
    m = jnp.asarray(np.concatenate([np.cos(ang), np.sin(ang)], axis=1), F32).astype(BF16)
    return pl.pallas_call(
        functools.partial(_fft2_kernel, tb=tb),
        grid=(b, s1 // tb),
        in_specs=[
            pl.BlockSpec((None, tb, s2, 2 * D_FOURIER), lambda i, j: (i, j, 0, 0)),
            pl.BlockSpec((s2, 2 * s2), lambda i, j: (0, 0)),
            pl.BlockSpec((1, D_FOURIER), lambda i, j: (0, 0)),
        ],
        out_specs=pl.BlockSpec((None, tb, s2, D_FOURIER), lambda i, j: (i, j, 0, 0)),
        out_shape=jax.ShapeDtypeStruct((b, s1, s2, D_FOURIER), BF16),
        compiler_params=_cparams("parallel", "parallel"),
        name="fft_stage2",
    )(v, m, g_fourier)


def _dft_dense_kernel(z_ref, cs_ref, ss_ref, g_ref, o_ref, *, chunk):
    g = g_ref[...]
    for c in range(o_ref.shape[0] // chunk):
        rows = slice(c * chunk, (c + 1) * chunk)
        y = jnp.dot(cs_ref[rows, :], z_ref[:, :D_FOURIER], preferred_element_type=F32)
        y += jnp.dot(ss_ref[rows, :], z_ref[:, D_FOURIER:], preferred_element_type=F32)
        o_ref[rows, :] = _rms_norm(y, g).astype(BF16)


def _dft_dense(proj3, g_fourier, b, s):
    k = np.arange(s)
    ang = 2.0 * np.pi * ((k[:, None] * k[None, :]) % s) / s
    cs = jnp.asarray(np.cos(ang), F32).astype(BF16)
    ss = jnp.asarray(np.sin(ang), F32).astype(BF16)
    return pl.pallas_call(
        functools.partial(_dft_dense_kernel, chunk=min(512, s)),
        grid=(b,),
        in_specs=[pl.BlockSpec((None, s, 2 * D_FOURIER), lambda i: (i, 0, 0)),
                  _resident((s, s)), _resident((s, s)), _resident((1, D_FOURIER))],
        out_specs=pl.BlockSpec((None, s, D_FOURIER), lambda i: (i, 0, 0)),
        out_shape=jax.ShapeDtypeStruct((b, s, D_FOURIER), BF16),
        compiler_params=_cparams("parallel"),
        name="dft_dense",
    )(proj3, cs, ss, g_fourier).reshape(b * s, D_FOURIER)


DENSE_DFT_MAX_SEQ = 2048


def _fourier_branch(proj, g_fourier, b, s):
    if s <= DENSE_DFT_MAX_SEQ:
        return _dft_dense(proj.reshape(b, s, D_PROJ), g_fourier, b, s)
    s1, s2 = FFT_S1, s // FFT_S1
    z = proj[:, :2 * D_FOURIER].reshape(b, s1, s2, 2 * D_FOURIER)
    y = _fft1(jnp.transpose(z, (0, 2, 1, 3)), s1, s2)
    o = _fft2(jnp.transpose(y, (0, 2, 1, 3)), g_fourier, s1, s2)
    return jnp.transpose(o, (0, 2, 1, 3)).reshape(b * s, D_FOURIER)


NA_ROWS_PER_STEP = 8


def _na_kernel(q_ref, k_ref, v_ref, bias_ref, o_ref, *, rows):
    nk = NA_KH * GRID_W
    lane = lax.broadcasted_iota(jnp.int32, (GRID_W, 2 * NA_HEAD_DIM), 1)
    first = lane < NA_HEAD_DIM
    scale = NA_HEAD_DIM ** -0.5
    head0 = (first.astype(F32) * scale).astype(BF16)
    head1 = ((1.0 - first.astype(F32)) * scale).astype(BF16)

    def body(i, carry):
        rr = [i * NA_ROWS_PER_STEP + u for u in range(NA_ROWS_PER_STEP)]
        rs = [jnp.clip(r - NA_KH // 2, 0, rows - NA_KH) for r in rr]
        tok = lambda r: pl.ds(pl.multiple_of(r * GRID_W, GRID_W), GRID_W)
        win = lambda r: pl.ds(pl.multiple_of(r * GRID_W, GRID_W), nk)
        scores = []
        for r, r0 in zip(rr, rs):
            q = q_ref[tok(r), :]
            qbd = jnp.concatenate([q * head0, q * head1], axis=0)
            s = lax.dot_general(qbd, k_ref[win(r0), :], (((1,), (1,)), ((), ())),
                                preferred_element_type=F32)
            scores.append(s + bias_ref[r - r0])
        probs = []
        for s in scores:
            p = jnp.exp(s - jnp.max(s, axis=-1, keepdims=True))
            probs.append((p.astype(BF16), jnp.sum(p, axis=-1, keepdims=True)))
        for r, r0, (p, l) in zip(rr, rs, probs):
            o = jnp.dot(p, v_ref[win(r0), :], preferred_element_type=F32) / l
            o_ref[tok(r), :] = jnp.where(first, o[:GRID_W], o[GRID_W:]).astype(BF16)
        return carry

    lax.fori_loop(0, rows // NA_ROWS_PER_STEP, body, 0)


def _na_bias_kernel(rpb_ref, o_ref):
    hp = pl.program_id(0)
    shape = (GRID_W, 2 * GRID_W)
    qi = lax.broadcasted_iota(jnp.int32, shape, 0)
    lane = lax.broadcasted_iota(jnp.int32, shape, 1)
    kk = lane & (GRID_W - 1)
    upper = lane >= GRID_W
    coff = jnp.clip(kk - qi, -(NA_KW - 1), NA_KW - 1) + NA_KW - 1
    c0 = jnp.clip(qi - NA_KW // 2, 0, GRID_W - NA_KW)
    valid = (kk >= c0) & (kk < c0 + NA_KW)
    n_ri, n_ci = 2 * NA_KH - 1, 2 * NA_KW - 1
    for hh in range(2):
        head = 2 * hp + hh

        def p_body(p, carry):
            def j_body(j, accs):
                out = []
                for ip in range(NA_KH // 2):
                    base = (head * n_ri + 2 * ip - p + NA_KH - 1) * n_ci + j
                    val = jnp.where(upper, rpb_ref[base + n_ci], rpb_ref[base])
                    out.append(jnp.where(coff == j, val, accs[ip]))
                return tuple(out)

            accs = lax.fori_loop(0, n_ci, j_body, (jnp.zeros(shape, F32),) * (NA_KH // 2))
            for ip in range(NA_KH // 2):
                o_ref[p, hh * GRID_W:(hh + 1) * GRID_W, ip * 128:(ip + 1) * 128] = jnp.where(
                    valid, accs[ip], MASK_VALUE)
            return carry

        lax.fori_loop(0, NA_KH, p_body, 0)


def _na_bias_table(rpb):
    hp = NA_HEADS // 2
    return pl.pallas_call(
        _na_bias_kernel,
        grid=(hp,),
        in_specs=[pl.BlockSpec(memory_space=pltpu.SMEM)],
        out_specs=pl.BlockSpec((None, NA_KH, 2 * GRID_W, NA_KH * GRID_W), lambda h: (h, 0, 0, 0)),
        out_shape=jax.ShapeDtypeStruct((hp, NA_KH, 2 * GRID_W, NA_KH * GRID_W), F32),
        compiler_params=_cparams("parallel"),
        name="na_bias_table",
    )(rpb.astype(F32).reshape(-1))


def _na_attention(proj3, bias, b, s):
    rows = s // GRID_W
    assert rows >= NA_KH and rows % NA_ROWS_PER_STEP == 0
    hp = NA_HEADS // 2
    q0 = 2 * D_FOURIER // 128
    blk = lambda off: pl.BlockSpec((None, s, 128), lambda h, i: (i, 0, off + h))
    return pl.pallas_call(
        functools.partial(_na_kernel, rows=rows),
        grid=(hp, b),
        in_specs=[
            blk(q0), blk(q0 + hp), blk(q0 + 2 * hp),
            pl.BlockSpec((None, NA_KH, 2 * GRID_W, NA_KH * GRID_W), lambda h, i: (h, 0, 0, 0)),
        ],
        out_specs=pl.BlockSpec((None, s, 128), lambda h, i: (i, 0, h)),
        out_shape=jax.ShapeDtypeStruct((b, s, D_NA), BF16),
        compiler_params=_cparams("parallel", "parallel"),
        name="na_attention",
    )(proj3, proj3, proj3, bias)


def _outproj_kernel(yf_ref, yn_ref, x_ref, ge_ref, be_ref, gna_ref, w_ref, g_ref, b_ref, wr_ref,
                    x1_ref, x1b_ref, aff_ref):
    half = x_ref.shape[0] // 2
    parts = [slice(0, half), slice(half, 2 * half)]
    mixes = []
    for rows in parts:
        yn = _rms_norm(yn_ref[rows, :].astype(F32), gna_ref[...]).astype(BF16)
        mix = jnp.dot(yf_ref[rows, :], w_ref[:D_FOURIER, :], preferred_element_type=F32)
        mixes.append(mix + jnp.dot(yn, w_ref[D_FOURIER:, :], preferred_element_type=F32))
    for rows, mix in zip(parts, mixes):
        x0 = _layer_norm(x_ref[rows, :], ge_ref[...], be_ref[...])
        x1 = _layer_norm(ALPHA * x0 + mix, g_ref[...], b_ref[...])
        x1_ref[rows, :] = x1
        x1b = x1.astype(BF16)
        x1b_ref[rows, :] = x1b
        logits = jnp.dot(x1b, wr_ref[...], preferred_element_type=F32)
        lane = lax.broadcasted_iota(jnp.int32, logits.shape, 1)
        logits = jnp.where(lane < N_EXPERTS, logits, MASK_VALUE)
        m = jnp.max(logits, axis=-1, keepdims=True)
        e = jnp.exp(logits - m)
        aff_ref[rows, :] = e / jnp.sum(e, axis=-1, keepdims=True)


def _outproj(yf, yn, x, ln_emb_g, ln_emb_b, g_na, w_out, ln_g, ln_b, w_router):
    n = x.shape[0]
    tm = min(512, n)
    row = lambda w: pl.BlockSpec((tm, w), lambda i: (i, 0))
    return pl.pallas_call(
        _outproj_kernel,
        grid=(n // tm,),
        in_specs=[row(D_FOURIER), row(D_NA), row(D_MODEL), _resident((1, D_MODEL)),
                  _resident((1, D_MODEL)), _resident((1, D_NA)), _resident((D_MODEL, D_MODEL)),
                  _resident((1, D_MODEL)), _resident((1, D_MODEL)), _resident((D_MODEL, 128))],
        out_specs=[row(D_MODEL), row(D_MODEL), row(128)],
        out_shape=[jax.ShapeDtypeStruct((n, D_MODEL), F32),
                   jax.ShapeDtypeStruct((n, D_MODEL), BF16),
                   jax.ShapeDtypeStruct((n, 128), F32)],
        compiler_params=_cparams("parallel"),
        name="outproj_ln_router",
    )(yf, yn, x, ln_emb_g, ln_emb_b, g_na, w_out, ln_g, ln_b, w_router)


def _tri(n, strict, lower):
    r = lax.broadcasted_iota(jnp.int32, (n, n), 0)
    c = lax.broadcasted_iota(jnp.int32, (n, n), 1)
    keep = (c < r if strict else c <= r) if lower else (r < c if strict else r <= c)
    return keep.astype(F32).astype(BF16)


def _prefix_counts(m3):
    e, r, _ = m3.shape
    cin = jnp.dot(m3.reshape(e * r, 128).astype(BF16), _tri(128, False, False),
                  preferred_element_type=F32).reshape(e, r, 128)
    tot = jnp.broadcast_to(cin[:, :, 127:128], (e, r, 128)).astype(BF16)
    low = _tri(r, True, True)
    roff = jnp.stack([jnp.dot(low, tot[i], preferred_element_type=F32) for i in range(e)])
    return cin, roff


def _count(m3):
    return jnp.sum(jnp.sum(m3, axis=1, keepdims=True), axis=2, keepdims=True)


def _select_kernel(aff_ref, sel_ref, slot_ref, *, cap):
    aff = aff_ref[...]
    e = aff.shape[0]

    def bisect(it, thr_bits):
        cand = thr_bits | (jnp.int32(1) << (30 - it))
        n_ge = _count((aff >= pltpu.bitcast(cand, F32)).astype(F32))
        return jnp.where(n_ge >= cap, cand, thr_bits)

    thr = pltpu.bitcast(lax.fori_loop(0, 31, bisect, jnp.zeros((e, 1, 1), jnp.int32)), F32)
    gt = (aff > thr).astype(F32)
    eq = (aff == thr).astype(F32)
    need = cap - _count(gt)
    cin, roff = _prefix_counts(eq)
    eq_rank = roff + cin - eq
    sel = jnp.maximum(gt, eq * (eq_rank < need).astype(F32))
    cin, roff = _prefix_counts(sel)
    sel_ref[...] = sel
    slot_ref[...] = (roff + cin - sel).astype(jnp.int32)


def _select(aff3, cap):
    e, r, _ = aff3.shape
    return pl.pallas_call(
        functools.partial(_select_kernel, cap=cap),
        out_shape=[jax.ShapeDtypeStruct((e, r, 128), F32),
                   jax.ShapeDtypeStruct((e, r, 128), jnp.int32)],
        compiler_params=pltpu.CompilerParams(vmem_limit_bytes=VMEM_LIMIT),
        name="expert_select",
    )(aff3)


def _index_kernel(sel_ref, idx_ref, *, cap):
    m = sel_ref[...]
    r = m.shape[0]
    mb = m.astype(BF16)
    cin = jnp.dot(mb, _tri(128, False, False), preferred_element_type=F32)
    ones = jnp.ones((8, 128), BF16)
    tot_row = lax.dot_general(ones, mb, (((1,), (1,)), ((), ())),
                              preferred_element_type=F32)
    end_row = jnp.dot(tot_row.astype(BF16), _tri(r, False, False),
                      preferred_element_type=F32)[0:1]
    off_row = end_row - tot_row[0:1]
    j = lax.broadcasted_iota(jnp.int32, (cap, r), 0).astype(F32)
    onehot = jnp.logical_and(j >= off_row, j < end_row).astype(F32).astype(BF16)
    row_id = lax.broadcasted_iota(jnp.int32, (r, 128), 0)
    tot = jnp.broadcast_to(cin[:, 127:128], (r, 128)).astype(BF16)
    roff = jnp.dot(_tri(r, True, True), tot, preferred_element_type=F32)
    pick = lambda mat: jnp.dot(onehot, mat.astype(BF16), preferred_element_type=F32)
    cin_j = pick(cin)
    row_j = pick(row_id.astype(F32))
    roff_hi = jnp.floor(roff * (1.0 / 128.0))
    roff_j = pick(roff_hi) * 128.0 + pick(roff - roff_hi * 128.0)
    jloc = lax.broadcasted_iota(jnp.int32, (cap, 128), 0).astype(F32) - roff_j
    lane_j = jnp.dot((cin_j <= jloc).astype(F32).astype(BF16), jnp.ones((128, 128), BF16),
                     preferred_element_type=F32)
    idx_ref[...] = (row_j * 128.0 + lane_j).astype(jnp.int32)


def _slot_index(sel3, cap):
    e, r, _ = sel3.shape
    return pl.pallas_call(
        functools.partial(_index_kernel, cap=cap),
        grid=(e,),
        in_specs=[pl.BlockSpec((None, r, 128), lambda i: (i, 0, 0))],
        out_specs=pl.BlockSpec((None, cap, 128), lambda i: (i, 0, 0)),
        out_shape=jax.ShapeDtypeStruct((e, cap, 128), jnp.int32),
        compiler_params=_cparams("parallel"),
        name="expert_slot_index",
    )(sel3)


def _expert_up_kernel(x_ref, wg_ref, wu_ref, h_ref, wgb_ref, wub_ref):
    @pl.when(pl.program_id(2) == 0)
    def _():
        wgb_ref[...] = wg_ref[...].astype(BF16)
        wub_ref[...] = wu_ref[...].astype(BF16)

    x = x_ref[...]
    g = jnp.dot(x, wgb_ref[...], preferred_element_type=F32)
    u = jnp.dot(x, wub_ref[...], preferred_element_type=F32)
    h_ref[...] = (g * jax.nn.sigmoid(g) * u).astype(BF16)


def _expert_down_kernel(h_ref, gate_ref, wd_ref, o_ref, wdb_ref):
    @pl.when(pl.program_id(2) == 0)
    def _():
        wdb_ref[...] = wd_ref[...].astype(BF16)

    y = jnp.dot(h_ref[...], wdb_ref[...], preferred_element_type=F32) * gate_ref[...]
    o_ref[...] = y.astype(BF16)


def _experts(xe, gate, w_gate, w_up, w_down):
    e, cap, d = xe.shape
    ff = w_gate.shape[-1]
    tm = min(1024, cap)
    tf = 512
    tn = 1024
    h = pl.pallas_call(
        _expert_up_kernel,
        grid=(e, ff // tf, cap // tm),
        in_specs=[
            pl.BlockSpec((None, tm, d), lambda i, f, j: (i, j, 0)),
            pl.BlockSpec((None, d, tf), lambda i, f, j: (i, 0, f)),
            pl.BlockSpec((None, d, tf), lambda i, f, j: (i, 0, f)),
        ],
        out_specs=pl.BlockSpec((None, tm, tf), lambda i, f, j: (i, j, f)),
        out_shape=jax.ShapeDtypeStruct((e, cap, ff), BF16),
        scratch_shapes=[pltpu.VMEM((d, tf), BF16), pltpu.VMEM((d, tf), BF16)],
        compiler_params=_cparams("parallel", "arbitrary", "arbitrary"),
        name="expert_up",
    )(xe, w_gate, w_up)
    return pl.pallas_call(
        _expert_down_kernel,
        grid=(e, d // tn, cap // tm),
        in_specs=[
            pl.BlockSpec((None, tm, ff), lambda i, f, j: (i, j, 0)),
            pl.BlockSpec((None, tm, 1), lambda i, f, j: (i, j, 0)),
            pl.BlockSpec((None, ff, tn), lambda i, f, j: (i, 0, f)),
        ],
        out_specs=pl.BlockSpec((None, tm, tn), lambda i, f, j: (i, j, f)),
        out_shape=jax.ShapeDtypeStruct((e, cap, d), BF16),
        scratch_shapes=[pltpu.VMEM((ff, tn), BF16)],
        compiler_params=_cparams("parallel", "arbitrary", "arbitrary"),
        name="expert_down",
    )(h, gate, w_down)


COMBINE_TOKENS = 256
COMBINE_SUBTILES = 2
COMBINE_WIN = 64
WIN_ALIGN = 16
NOT_SELECTED = -(1 << 30)


def _final_kernel(woff_ref, npass_ref, x1_ref, x1b_ref, slot_ref, p_ref, wpp_ref, wpg_ref, g_ref,
                  b_ref, ye_ref, o_ref, win_ref, sem, *, cap, subtiles):
    i = pl.program_id(0)
    t = x1_ref.shape[0] // subtiles
    lane = lax.broadcasted_iota(jnp.int32, (t, 2 * COMBINE_WIN), 1)
    upper = lane >= COMBINE_WIN
    col = lane & (COMBINE_WIN - 1)

    def window_start(h, e, q):
        lo = woff_ref[i * subtiles + h, e] + q * COMBINE_WIN
        return lo, pl.multiple_of(jnp.minimum(lo, cap - COMBINE_WIN), WIN_ALIGN)

    def window_copy(h, e, q):
        _, start = window_start(h, e, q)
        return pltpu.make_async_copy(ye_ref.at[e, pl.ds(start, COMBINE_WIN)],
                                     win_ref.at[h, pl.ds(e * COMBINE_WIN, COMBINE_WIN)], sem.at[h])

    def start_windows(h, q):
        for e in range(N_EXPERTS):
            window_copy(h, e, q).start()

    def wait_windows(h, q):
        for e in range(N_EXPERTS):
            window_copy(h, e, q).wait()

    def routed(h, q):
        slot = slot_ref[h * t:(h + 1) * t, :]
        blocks = []
        for pair in range(N_EXPERTS // 2):
            e0, e1 = 2 * pair, 2 * pair + 1
            lo0, st0 = window_start(h, e0, q)
            lo1, st1 = window_start(h, e1, q)
            rel = jnp.where(upper, slot[:, e1:e1 + 1] - lo1, slot[:, e0:e0 + 1] - lo0)
            shift = jnp.where(upper, lo1 - st1, lo0 - st0)
            hit = (rel >= 0) & (rel < COMBINE_WIN) & (col == rel + shift)
            blocks.append(hit.astype(F32).astype(BF16))
        route = jnp.concatenate(blocks, axis=1)
        return jnp.dot(route, win_ref[h], preferred_element_type=F32)

    for h in range(subtiles):
        start_windows(h, 0)
    bases = []
    for h in range(subtiles):
        rows = slice(h * t, (h + 1) * t)
        pp = jnp.dot(p_ref[rows, :].astype(BF16), wpp_ref[...], preferred_element_type=F32)
        pg = jnp.dot(x1b_ref[rows, :], wpg_ref[...], preferred_element_type=F32)
        bases.append(ALPHA * x1_ref[rows, :] + pp * jax.nn.sigmoid(pg))
    moes = []
    for h in range(subtiles):
        wait_windows(h, 0)
        moes.append(routed(h, 0))
    for h in range(subtiles):
        def extra_pass(q, moe, h=h):
            start_windows(h, q)
            wait_windows(h, q)
            return moe + routed(h, q)

        n_pass = jnp.maximum(npass_ref[i * subtiles + h], 1)
        moe = lax.fori_loop(1, n_pass, extra_pass, moes[h])
        o_ref[h * t:(h + 1) * t, :] = _layer_norm(bases[h] + moe, g_ref[...], b_ref[...])


def _final(x1, x1b, slot_tok, woff, npass, ye, p, w_pp, w_pg, ln_g, ln_b):
    n = x1.shape[0]
    cap = ye.shape[1]
    sub = COMBINE_SUBTILES if n % (COMBINE_SUBTILES * COMBINE_TOKENS) == 0 else 1
    tm = sub * min(COMBINE_TOKENS, n)
    assert cap >= COMBINE_WIN and cap % WIN_ALIGN == 0
    row = lambda w: pl.BlockSpec((tm, w), lambda i, *_: (i, 0))
    return pl.pallas_call(
        functools.partial(_final_kernel, cap=cap, subtiles=sub),
        grid_spec=pltpu.PrefetchScalarGridSpec(
            num_scalar_prefetch=2,
            grid=(n // tm,),
            in_specs=[row(D_MODEL), row(D_MODEL), row(N_EXPERTS), row(PLE_DIM),
                      _resident((PLE_DIM, D_MODEL)), _resident((D_MODEL, D_MODEL)),
                      _resident((1, D_MODEL)), _resident((1, D_MODEL)),
                      pl.BlockSpec(memory_space=pl.ANY)],
            out_specs=row(D_MODEL),
            scratch_shapes=[pltpu.VMEM((sub, N_EXPERTS * COMBINE_WIN, D_MODEL), BF16),
                            pltpu.SemaphoreType.DMA((sub,))],
        ),
        out_shape=jax.ShapeDtypeStruct((n, D_MODEL), F32),
        compiler_params=_cparams("arbitrary"),
        name="ple_combine_final_ln",
    )(woff, npass, x1, x1b, slot_tok, p, w_pp, w_pg, ln_g, ln_b, ye)


def _dispatch_tables(sel3, slot3, n, cap):
    tm = min(COMBINE_TOKENS, n)
    sel = sel3.reshape(N_EXPERTS, n) > 0
    slot = slot3.reshape(N_EXPERTS, n)
    slot_tok = jnp.where(sel, slot, NOT_SELECTED).T
    first = slot[:, ::tm]
    last = jnp.concatenate([first[:, 1:], jnp.full((N_EXPERTS, 1), cap, jnp.int32)], axis=1)
    woff = (first // WIN_ALIGN) * WIN_ALIGN
    passes = jnp.where(last > first, (last - woff + COMBINE_WIN - 1) // COMBINE_WIN, 0)
    return slot_tok, woff.T, jnp.max(passes, axis=0).astype(jnp.int32)


def _trunk(x, p, wts):
    b, s, _ = x.shape
    n = b * s
    x2 = x.reshape(n, D_MODEL)
    proj = _ln_inproj(x2, wts["ln_emb_g"], wts["ln_emb_b"], wts["w_proj"])
    yf = _fourier_branch(proj, wts["g_fourier"], b, s)
    yn = _na_attention(proj.reshape(b, s, D_PROJ), wts["na_bias"], b, s).reshape(n, D_NA)
    x1, x1b, aff = _outproj(yf, yn, x2, wts["ln_emb_g"], wts["ln_emb_b"], wts["g_na"], wts["w_out"],
                            wts["ln1_g"], wts["ln1_b"], wts["w_router"])
    cap = EC_CAPACITY_FACTOR * n // N_EXPERTS
    aff_t = aff[:, :N_EXPERTS].T
    sel3, slot3 = _select(aff_t.reshape(N_EXPERTS, n // 128, 128), cap)
    idx = _slot_index(sel3, cap)[:, :, 0]
    gate = jnp.take_along_axis(aff_t, idx, axis=1)
    xe = x1b[idx]
    ye = _experts(xe, gate[..., None], wts["w_gate"], wts["w_up"], wts["w_down"])
    slot_tok, woff, npass = _dispatch_tables(sel3, slot3, n, cap)
    y = _final(x1, x1b, slot_tok, woff, npass, ye, p.reshape(n, PLE_DIM), wts["w_ple_proj"],
               wts["w_ple_gate"], wts["ln2_g"], wts["ln2_b"])
    return y.reshape(b, s, D_MODEL)


def _prepare_weights(ln_emb_g, ln_emb_b, w_in, w_fourier, rpb, g_fourier, g_na, w_out, ln1_g, ln1_b,
                     w_router, w_gate, w_up, w_down, w_ple_proj, w_ple_gate, ln2_g, ln2_b):
    assert w_in.shape[0] == 1, "one encoder layer"
    w_zr, w_zi = _fold_fourier(w_in[0], w_fourier[0])
    row = lambda a: a.reshape(1, -1).astype(F32)
    return {
        "ln_emb_g": row(ln_emb_g), "ln_emb_b": row(ln_emb_b),
        "w_proj": jnp.concatenate([w_zr, w_zi, w_in[0][:, D_FOURIER:].astype(BF16)], axis=1),
        "na_bias": _na_bias_table(rpb[0]),
        "g_fourier": row(g_fourier[0]), "g_na": row(g_na[0]),
        "w_out": w_out[0].astype(BF16),
        "ln1_g": row(ln1_g[0]), "ln1_b": row(ln1_b[0]),
        "w_router": jnp.pad(w_router[0], ((0, 0), (0, 128 - N_EXPERTS))).astype(BF16),
        "w_gate": w_gate[0], "w_up": w_up[0], "w_down": w_down[0],
        "w_ple_proj": w_ple_proj[0].astype(BF16), "w_ple_gate": w_ple_gate[0].astype(BF16),
        "ln2_g": row(ln2_g[0]), "ln2_b": row(ln2_b[0]),
    }


def kernel(x_prompt, x_sample, p_prompt, p_sample, ln_emb_g, ln_emb_b, w_in, w_fourier, rpb,
           g_fourier, g_na, w_out, ln1_g, ln1_b, w_router, w_gate, w_up, w_down,
           w_ple_proj, w_ple_gate, ln2_g, ln2_b):
    wts = _prepare_weights(ln_emb_g, ln_emb_b, w_in, w_fourier, rpb, g_fourier, g_na, w_out, ln1_g,
                           ln1_b, w_router, w_gate, w_up, w_down, w_ple_proj, w_ple_gate, ln2_g, ln2_b)
    return (_trunk(x_prompt, p_prompt[0], wts), _trunk(x_sample, p_sample[0], wts))
```

```python
import functools

import numpy as np
import jax
import jax.numpy as jnp
from jax import lax
from jax.experimental import pallas as pl
from jax.experimental.pallas import tpu as pltpu

D_MODEL = 2048
GRID_W = 64
D_FOURIER = 1024
FOURIER_GROUPS = 4
FOURIER_GW = 256
D_NA = 1024
NA_HEADS = 16
NA_HEAD_DIM = 64
NA_KH = 8
NA_KW = 16
N_EXPERTS = 16
EC_CAPACITY_FACTOR = 2
PLE_DIM = 256
LN_EPS = 1e-5
RMS_EPS = 1e-6
ALPHA = float(2.0 ** 0.25)
D_PROJ = 2 * D_FOURIER + 3 * D_NA
MASK_VALUE = -1e30
VMEM_LIMIT = 56 * 1024 * 1024
FFT_S1 = 128

BF16 = jnp.bfloat16
F32 = jnp.float32


def _cparams(*sem):
    return pltpu.CompilerParams(dimension_semantics=sem, vmem_limit_bytes=VMEM_LIMIT)


def _resident(shape):
    return pl.BlockSpec(shape, lambda *_: (0,) * len(shape), pipeline_mode=pl.Buffered(1))


def _layer_norm(x, g, b):
    mu = jnp.mean(x, axis=-1, keepdims=True)
    xc = x - mu
    var = jnp.mean(xc * xc, axis=-1, keepdims=True)
    return xc * lax.rsqrt(var + LN_EPS) * g + b


def _rms_norm(x, g):
    return x * lax.rsqrt(jnp.mean(x * x, axis=-1, keepdims=True) + RMS_EPS) * g


def _fold_kernel(win_ref, wf_ref, cc_ref, sc_ref, zr_ref, zi_ref):
    hi = lax.Precision.HIGHEST
    wf = wf_ref[...]
    a = jnp.dot(cc_ref[...], wf, precision=hi, preferred_element_type=F32)
    b = jnp.dot(sc_ref[...], wf, precision=hi, preferred_element_type=F32)
    w = win_ref[...]
    zr_ref[...] = jnp.dot(w, a, precision=hi, preferred_element_type=F32).astype(BF16)
    zi_ref[...] = (-jnp.dot(w, b, precision=hi, preferred_element_type=F32)).astype(BF16)


def _fold_fourier(w_in, w_fourier):
    c = np.arange(FOURIER_GW)
    ang = 2.0 * np.pi * ((c[:, None] * c[None, :]) % FOURIER_GW) / FOURIER_GW
    cc = jnp.asarray(np.cos(ang), F32)
    sc = jnp.asarray(np.sin(ang), F32)
    gw = FOURIER_GW
    return pl.pallas_call(
        _fold_kernel,
        grid=(FOURIER_GROUPS,),
        in_specs=[
            pl.BlockSpec((D_MODEL, gw), lambda g: (0, g)),
            pl.BlockSpec((None, gw, gw), lambda g: (g, 0, 0)),
            pl.BlockSpec((gw, gw), lambda g: (0, 0)),
            pl.BlockSpec((gw, gw), lambda g: (0, 0)),
        ],
        out_specs=[
            pl.BlockSpec((D_MODEL, gw), lambda g: (0, g)),
            pl.BlockSpec((D_MODEL, gw), lambda g: (0, g)),
        ],
        out_shape=[jax.ShapeDtypeStruct((D_MODEL, D_FOURIER), BF16)] * 2,
        compiler_params=_cparams("arbitrary"),
        name="fold_fourier",
    )(w_in, w_fourier, cc, sc)


def _ln_inproj_kernel(x_ref, g_ref, b_ref, w_ref, proj_ref, xn_ref):
    @pl.when(pl.program_id(1) == 0)
    def _():
        xn_ref[...] = _layer_norm(x_ref[...], g_ref[...], b_ref[...]).astype(BF16)

    proj_ref[...] = jnp.dot(xn_ref[...], w_ref[...], preferred_element_type=F32).astype(BF16)


def _ln_inproj(x, g, b, w):
    n = x.shape[0]
    tm = min(1024, n)
    tn = 1024
    return pl.pallas_call(
        _ln_inproj_kernel,
        grid=(n // tm, D_PROJ // tn),
        in_specs=[
            pl.BlockSpec((tm, D_MODEL), lambda i, j: (i, 0)),
            pl.BlockSpec((1, D_MODEL), lambda i, j: (0, 0)),
            pl.BlockSpec((1, D_MODEL), lambda i, j: (0, 0)),
            pl.BlockSpec((D_MODEL, tn), lambda i, j: (0, j)),
        ],
        out_specs=pl.BlockSpec((tm, tn), lambda i, j: (i, j)),
        out_shape=jax.ShapeDtypeStruct((n, D_PROJ), BF16),
        scratch_shapes=[pltpu.VMEM((tm, D_MODEL), BF16)],
        compiler_params=_cparams("parallel", "arbitrary"),
        name="ln_inproj",
    )(x, g, b, w)


def _fft1_kernel(z_ref, m_ref, y_ref, *, tb):
    for t in range(tb):
        x = jnp.concatenate([z_ref[t, :, :D_FOURIER], z_ref[t, :, D_FOURIER:]], axis=0)
        y = jnp.dot(m_ref[t], x, preferred_element_type=F32)
        s1 = y.shape[0] // 2
        y_ref[t, :, :D_FOURIER] = y[:s1].astype(BF16)
        y_ref[t, :, D_FOURIER:] = y[s1:].astype(BF16)


def _fft1_matrices(s1, s2):
    s = s1 * s2
    k1 = np.arange(s1)[None, :, None]
    n = (s2 * np.arange(s1)[None, None, :] + np.arange(s2)[:, None, None])
    ang = 2.0 * np.pi * ((k1 * n) % s) / s
    fr, fi = np.cos(ang), -np.sin(ang)
    m = np.concatenate([np.concatenate([fr, -fi], axis=2),
                        np.concatenate([fi, fr], axis=2)], axis=1)
    return jnp.asarray(m, F32).astype(BF16)


def _fft1(zt, s1, s2):
    b = zt.shape[0]
    tb = min(4, s2)
    m = _fft1_matrices(s1, s2)
    return pl.pallas_call(
        functools.partial(_fft1_kernel, tb=tb),
        grid=(s2 // tb, b),
        in_specs=[
            pl.BlockSpec((None, tb, s1, 2 * D_FOURIER), lambda j, i: (i, j, 0, 0)),
            pl.BlockSpec((tb, 2 * s1, 2 * s1), lambda j, i: (j, 0, 0)),
        ],
        out_specs=pl.BlockSpec((None, tb, s1, 2 * D_FOURIER), lambda j, i: (i, j, 0, 0)),
        out_shape=jax.ShapeDtypeStruct(zt.shape, BF16),
        compiler_params=_cparams("parallel", "parallel"),
        name="fft_stage1",
    )(zt, m)


def _fft2_kernel(v_ref, m_ref, g_ref, o_ref, *, tb):
    m = m_ref[...]
    g = g_ref[...]
    for t in range(tb):
        x = jnp.concatenate([v_ref[t, :, :D_FOURIER], v_ref[t, :, D_FOURIER:]], axis=0)
        y = jnp.dot(m, x, preferred_element_type=F32)
        o_ref[t] = _rms_norm(y, g).astype(BF16)


def _fft2(v, g_fourier, s1, s2):
    b = v.shape[0]
    tb = 8
    k2 = np.arange(s2)
    ang = 2.0 * np.pi * ((k2[:, None] * k2[None, :]) % s2) / s2
    m = jnp.asarray(np.concatenate([np.cos(ang), np.sin(ang)], axis=1), F32).astype(BF16)
    return pl.pallas_call(
        functools.partial(_fft2_kernel, tb=tb),
        grid=(b, s1 // tb),
        in_specs=[
            pl.BlockSpec((None, tb, s2, 2 * D_FOURIER), lambda i, j: (i, j, 0, 0)),
            pl.BlockSpec((s2, 2 * s2), lambda i, j: (0, 0)),
            pl.BlockSpec((1, D_FOURIER), lambda i, j: (0, 0)),
        ],
        out_specs=pl.BlockSpec((None, tb, s2, D_FOURIER), lambda i, j: (i, j, 0, 0)),
        out_shape=jax.ShapeDtypeStruct((b, s1, s2, D_FOURIER), BF16),
        compiler_params=_cparams("parallel", "parallel"),
        name="fft_stage2",
    )(v, m, g_fourier)


def _dft_dense_kernel(z_ref, cs_ref, ss_ref, g_ref, o_ref, *, chunk):
    g = g_ref[...]
    for c in range(o_ref.shape[0] // chunk):
        rows = slice(c * chunk, (c + 1) * chunk)
        y = jnp.dot(cs_ref[rows, :], z_ref[:, :D_FOURIER], preferred_element_type=F32)
        y += jnp.dot(ss_ref[rows, :], z_ref[:, D_FOURIER:], preferred_element_type=F32)
        o_ref[rows, :] = _rms_norm(y, g).astype(BF16)


def _dft_dense(proj3, g_fourier, b, s):
    k = np.arange(s)
    ang = 2.0 * np.pi * ((k[:, None] * k[None, :]) % s) / s
    cs = jnp.asarray(np.cos(ang), F32).astype(BF16)
    ss = jnp.asarray(np.sin(ang), F32).astype(BF16)
    return pl.pallas_call(
        functools.partial(_dft_dense_kernel, chunk=min(512, s)),
        grid=(b,),
        in_specs=[pl.BlockSpec((None, s, 2 * D_FOURIER), lambda i: (i, 0, 0)),
                  _resident((s, s)), _resident((s, s)), _resident((1, D_FOURIER))],
        out_specs=pl.BlockSpec((None, s, D_FOURIER), lambda i: (i, 0, 0)),
        out_shape=jax.ShapeDtypeStruct((b, s, D_FOURIER), BF16),
        compiler_params=_cparams("parallel"),
        name="dft_dense",
    )(proj3, cs, ss, g_fourier).reshape(b * s, D_FOURIER)


DENSE_DFT_MAX_SEQ = 2048


def _fourier_branch(proj, g_fourier, b, s):
    if s <= DENSE_DFT_MAX_SEQ:
        return _dft_dense(proj.reshape(b, s, D_PROJ), g_fourier, b, s)
    s1, s2 = FFT_S1, s // FFT_S1
    z = proj[:, :2 * D_FOURIER].reshape(b, s1, s2, 2 * D_FOURIER)
    y = _fft1(jnp.transpose(z, (0, 2, 1, 3)), s1, s2)
    o = _fft2(jnp.transpose(y, (0, 2, 1, 3)), g_fourier, s1, s2)
    return jnp.transpose(o, (0, 2, 1, 3)).reshape(b * s, D_FOURIER)


NA_ROWS_PER_STEP = 8


def _na_kernel(q_ref, k_ref, v_ref, bias_ref, o_ref, *, rows):
    nk = NA_KH * GRID_W
    lane = lax.broadcasted_iota(jnp.int32, (GRID_W, 2 * NA_HEAD_DIM), 1)
    first = lane < NA_HEAD_DIM
    scale = NA_HEAD_DIM ** -0.5
    head0 = (first.astype(F32) * scale).astype(BF16)
    head1 = ((1.0 - first.astype(F32)) * scale).astype(BF16)

    def body(i, carry):
        rr = [i * NA_ROWS_PER_STEP + u for u in range(NA_ROWS_PER_STEP)]
        rs = [jnp.clip(r - NA_KH // 2, 0, rows - NA_KH) for r in rr]
        tok = lambda r: pl.ds(pl.multiple_of(r * GRID_W, GRID_W), GRID_W)
        win = lambda r: pl.ds(pl.multiple_of(r * GRID_W, GRID_W), nk)
        scores = []
        for r, r0 in zip(rr, rs):
            q = q_ref[tok(r), :]
            qbd = jnp.concatenate([q * head0, q * head1], axis=0)
            s = lax.dot_general(qbd, k_ref[win(r0), :], (((1,), (1,)), ((), ())),
                                preferred_element_type=F32)
            scores.append(s + bias_ref[r - r0])
        probs = []
        for s in scores:
            p = jnp.exp(s - jnp.max(s, axis=-1, keepdims=True))
            probs.append((p.astype(BF16), jnp.sum(p, axis=-1, keepdims=True)))
        for r, r0, (p, l) in zip(rr, rs, probs):
            o = jnp.dot(p, v_ref[win(r0), :], preferred_element_type=F32) / l
            o_ref[tok(r), :] = jnp.where(first, o[:GRID_W], o[GRID_W:]).astype(BF16)
        return carry

    lax.fori_loop(0, rows // NA_ROWS_PER_STEP, body, 0)


def _na_bias_kernel(rpb_ref, o_ref):
    hp = pl.program_id(0)
    shape = (GRID_W, 2 * GRID_W)
    qi = lax.broadcasted_iota(jnp.int32, shape, 0)
    lane = lax.broadcasted_iota(jnp.int32, shape, 1)
    kk = lane & (GRID_W - 1)
    upper = lane >= GRID_W
    coff = jnp.clip(kk - qi, -(NA_KW - 1), NA_KW - 1) + NA_KW - 1
    c0 = jnp.clip(qi - NA_KW // 2, 0, GRID_W - NA_KW)
    valid = (kk >= c0) & (kk < c0 + NA_KW)
    n_ri, n_ci = 2 * NA_KH - 1, 2 * NA_KW - 1
    for hh in range(2):
        head = 2 * hp + hh

        def p_body(p, carry):
            def j_body(j, accs):
                out = []
                for ip in range(NA_KH // 2):
                    base = (head * n_ri + 2 * ip - p + NA_KH - 1) * n_ci + j
                    val = jnp.where(upper, rpb_ref[base + n_ci], rpb_ref[base])
                    out.append(jnp.where(coff == j, val, accs[ip]))
                return tuple(out)

            accs = lax.fori_loop(0, n_ci, j_body, (jnp.zeros(shape, F32),) * (NA_KH // 2))
            for ip in range(NA_KH // 2):
                o_ref[p, hh * GRID_W:(hh + 1) * GRID_W, ip * 128:(ip + 1) * 128] = jnp.where(
                    valid, accs[ip], MASK_VALUE)
            return carry

        lax.fori_loop(0, NA_KH, p_body, 0)


def _na_bias_table(rpb):
    hp = NA_HEADS // 2
    return pl.pallas_call(
        _na_bias_kernel,
        grid=(hp,),
        in_specs=[pl.BlockSpec(memory_space=pltpu.SMEM)],
        out_specs=pl.BlockSpec((None, NA_KH, 2 * GRID_W, NA_KH * GRID_W), lambda h: (h, 0, 0, 0)),
        out_shape=jax.ShapeDtypeStruct((hp, NA_KH, 2 * GRID_W, NA_KH * GRID_W), F32),
        compiler_params=_cparams("parallel"),
        name="na_bias_table",
    )(rpb.astype(F32).reshape(-1))


def _na_attention(proj3, bias, b, s):
    rows = s // GRID_W
    assert rows >= NA_KH and rows % NA_ROWS_PER_STEP == 0
    hp = NA_HEADS // 2
    q0 = 2 * D_FOURIER // 128
    blk = lambda off: pl.BlockSpec((None, s, 128), lambda h, i: (i, 0, off + h))
    return pl.pallas_call(
        functools.partial(_na_kernel, rows=rows),
        grid=(hp, b),
        in_specs=[
            blk(q0), blk(q0 + hp), blk(q0 + 2 * hp),
            pl.BlockSpec((None, NA_KH, 2 * GRID_W, NA_KH * GRID_W), lambda h, i: (h, 0, 0, 0)),
        ],
        out_specs=pl.BlockSpec((None, s, 128), lambda h, i: (i, 0, h)),
        out_shape=jax.ShapeDtypeStruct((b, s, D_NA), BF16),
        compiler_params=_cparams("parallel", "parallel"),
        name="na_attention",
    )(proj3, proj3, proj3, bias)


def _outproj_kernel(yf_ref, yn_ref, x_ref, ge_ref, be_ref, gna_ref, w_ref, g_ref, b_ref, wr_ref,
                    x1_ref, x1b_ref, aff_ref):
    half = x_ref.shape[0] // 2
    parts = [slice(0, half), slice(half, 2 * half)]
    mixes = []
    for rows in parts:
        yn = _rms_norm(yn_ref[rows, :].astype(F32), gna_ref[...]).astype(BF16)
        mix = jnp.dot(yf_ref[rows, :], w_ref[:D_FOURIER, :], preferred_element_type=F32)
        mixes.append(mix + jnp.dot(yn, w_ref[D_FOURIER:, :], preferred_element_type=F32))
    for rows, mix in zip(parts, mixes):
        x0 = _layer_norm(x_ref[rows, :], ge_ref[...], be_ref[...])
        x1 = _layer_norm(ALPHA * x0 + mix, g_ref[...], b_ref[...])
        x1_ref[rows, :] = x1
        x1b = x1.astype(BF16)
        x1b_ref[rows, :] = x1b
        logits = jnp.dot(x1b, wr_ref[...], preferred_element_type=F32)
        lane = lax.broadcasted_iota(jnp.int32, logits.shape, 1)
        logits = jnp.where(lane < N_EXPERTS, logits, MASK_VALUE)
        m = jnp.max(logits, axis=-1, keepdims=True)
        e = jnp.exp(logits - m)
        aff_ref[rows, :] = e / jnp.sum(e, axis=-1, keepdims=True)


def _outproj(yf, yn, x, ln_emb_g, ln_emb_b, g_na, w_out, ln_g, ln_b, w_router):
    n = x.shape[0]
    tm = min(512, n)
    row = lambda w: pl.BlockSpec((tm, w), lambda i: (i, 0))
    return pl.pallas_call(
        _outproj_kernel,
        grid=(n // tm,),
        in_specs=[row(D_FOURIER), row(D_NA), row(D_MODEL), _resident((1, D_MODEL)),
                  _resident((1, D_MODEL)), _resident((1, D_NA)), _resident((D_MODEL, D_MODEL)),
                  _resident((1, D_MODEL)), _resident((1, D_MODEL)), _resident((D_MODEL, 128))],
        out_specs=[row(D_MODEL), row(D_MODEL), row(128)],
        out_shape=[jax.ShapeDtypeStruct((n, D_MODEL), F32),
                   jax.ShapeDtypeStruct((n, D_MODEL), BF16),
                   jax.ShapeDtypeStruct((n, 128), F32)],
        compiler_params=_cparams("parallel"),
        name="outproj_ln_router",
    )(yf, yn, x, ln_emb_g, ln_emb_b, g_na, w_out, ln_g, ln_b, w_router)


def _tri(n, strict, lower):
    r = lax.broadcasted_iota(jnp.int32, (n, n), 0)
    c = lax.broadcasted_iota(jnp.int32, (n, n), 1)
    keep = (c < r if strict else c <= r) if lower else (r < c if strict else r <= c)
    return keep.astype(F32).astype(BF16)


def _prefix_counts(m3):
    e, r, _ = m3.shape
    cin = jnp.dot(m3.reshape(e * r, 128).astype(BF16), _tri(128, False, False),
                  preferred_element_type=F32).reshape(e, r, 128)
    tot = jnp.broadcast_to(cin[:, :, 127:128], (e, r, 128)).astype(BF16)
    low = _tri(r, True, True)
    roff = jnp.stack([jnp.dot(low, tot[i], preferred_element_type=F32) for i in range(e)])
    return cin, roff


def _count(m3):
    return jnp.sum(jnp.sum(m3, axis=1, keepdims=True), axis=2, keepdims=True)


def _select_kernel(aff_ref, sel_ref, slot_ref, *, cap):
    aff = aff_ref[...]
    e = aff.shape[0]

    def bisect(it, thr_bits):
        cand = thr_bits | (jnp.int32(1) << (30 - it))
        n_ge = _count((aff >= pltpu.bitcast(cand, F32)).astype(F32))
        return jnp.where(n_ge >= cap, cand, thr_bits)

    thr = pltpu.bitcast(lax.fori_loop(0, 31, bisect, jnp.zeros((e, 1, 1), jnp.int32)), F32)
    gt = (aff > thr).astype(F32)
    eq = (aff == thr).astype(F32)
    need = cap - _count(gt)
    cin, roff = _prefix_counts(eq)
    eq_rank = roff + cin - eq
    sel = jnp.maximum(gt, eq * (eq_rank < need).astype(F32))
    cin, roff = _prefix_counts(sel)
    sel_ref[...] = sel
    slot_ref[...] = (roff + cin - sel).astype(jnp.int32)


def _select(aff3, cap):
    e, r, _ = aff3.shape
    return pl.pallas_call(
        functools.partial(_select_kernel, cap=cap),
        out_shape=[jax.ShapeDtypeStruct((e, r, 128), F32),
                   jax.ShapeDtypeStruct((e, r, 128), jnp.int32)],
        compiler_params=pltpu.CompilerParams(vmem_limit_bytes=VMEM_LIMIT),
        name="expert_select",
    )(aff3)


def _index_kernel(sel_ref, aff_ref, idx_ref, gate_ref, *, cap):
    m = sel_ref[...]
    r = m.shape[0]
    mb = m.astype(BF16)
    cin = jnp.dot(mb, _tri(128, False, False), preferred_element_type=F32)
    ones = jnp.ones((8, 128), BF16)
    tot_row = lax.dot_general(ones, mb, (((1,), (1,)), ((), ())),
                              preferred_element_type=F32)
    end_row = jnp.dot(tot_row.astype(BF16), _tri(r, False, False),
                      preferred_element_type=F32)[0:1]
    off_row = end_row - tot_row[0:1]
    j = lax.broadcasted_iota(jnp.int32, (cap, r), 0).astype(F32)
    onehot = jnp.logical_and(j >= off_row, j < end_row).astype(F32).astype(BF16)
    row_id = lax.broadcasted_iota(jnp.int32, (r, 128), 0)
    tot = jnp.broadcast_to(cin[:, 127:128], (r, 128)).astype(BF16)
    roff = jnp.dot(_tri(r, True, True), tot, preferred_element_type=F32)
    pick = lambda mat: jnp.dot(onehot, mat.astype(BF16), preferred_element_type=F32)
    cin_j = pick(cin)
    row_j = pick(row_id.astype(F32))
    roff_hi = jnp.floor(roff * (1.0 / 128.0))
    roff_j = pick(roff_hi) * 128.0 + pick(roff - roff_hi * 128.0)
    jloc = lax.broadcasted_iota(jnp.int32, (cap, 128), 0).astype(F32) - roff_j
    lane_j = jnp.dot((cin_j <= jloc).astype(F32).astype(BF16), jnp.ones((128, 128), BF16),
                     preferred_element_type=F32)
    idx_ref[...] = (row_j * 128.0 + lane_j).astype(jnp.int32)
    aff_j = jnp.dot(onehot.astype(F32), aff_ref[...], precision=lax.Precision.HIGHEST,
                    preferred_element_type=F32)
    lane = lax.broadcasted_iota(jnp.int32, (cap, 128), 1).astype(F32)
    gate = jnp.sum(jnp.where(lane == lane_j, aff_j, 0.0), axis=1, keepdims=True)
    gate_ref[...] = jnp.broadcast_to(gate, (cap, 128))


def _slot_index(sel3, aff3, cap):
    e, r, _ = sel3.shape
    blk = pl.BlockSpec((None, r, 128), lambda i: (i, 0, 0))
    out = pl.BlockSpec((None, cap, 128), lambda i: (i, 0, 0))
    return pl.pallas_call(
        functools.partial(_index_kernel, cap=cap),
        grid=(e,),
        in_specs=[blk, blk],
        out_specs=[out, out],
        out_shape=[jax.ShapeDtypeStruct((e, cap, 128), jnp.int32),
                   jax.ShapeDtypeStruct((e, cap, 128), F32)],
        compiler_params=_cparams("parallel"),
        name="expert_slot_index",
    )(sel3, aff3)


GATHER_ROWS = 256


def _gather_kernel(idx_ref, x_ref, o_ref, buf_ref, sem):
    j = pl.program_id(1)
    slot = j % 2
    rows = o_ref.shape[0]

    def row_copy(tile, r, sl):
        tok = idx_ref[0, tile * rows + r]
        return pltpu.make_async_copy(x_ref.at[pl.ds(tok, 1)], buf_ref.at[sl, pl.ds(r, 1)], sem.at[sl])

    def request(tile, sl):
        def body(r, carry):
            row_copy(tile, r, sl).start()
            return carry
        lax.fori_loop(0, rows, body, 0, unroll=8)

    @pl.when(j == 0)
    def _():
        request(0, 0)

    @pl.when(j + 1 < pl.num_programs(1))
    def _():
        request(j + 1, 1 - slot)

    def drain(r, carry):
        row_copy(j, r, slot).wait()
        return carry
    lax.fori_loop(0, rows, drain, 0, unroll=8)
    o_ref[...] = buf_ref[slot].astype(BF16)


def _gather_rows(x1, idx):
    e, cap = idx.shape
    d = x1.shape[1]
    tr = min(GATHER_ROWS, cap)
    assert cap % tr == 0 and tr % 8 == 0
    return pl.pallas_call(
        _gather_kernel,
        grid=(e, cap // tr),
        in_specs=[pl.BlockSpec((None, 1, cap), lambda i, j: (i, 0, 0), memory_space=pltpu.SMEM),
                  pl.BlockSpec(memory_space=pl.ANY)],
        out_specs=pl.BlockSpec((None, tr, d), lambda i, j: (i, j, 0)),
        out_shape=jax.ShapeDtypeStruct((e, cap, d), BF16),
        scratch_shapes=[pltpu.VMEM((2, tr, d), F32), pltpu.SemaphoreType.DMA((2,))],
        compiler_params=_cparams("arbitrary", "arbitrary"),
        name="expert_gather",
    )(idx.reshape(e, 1, cap), x1)


def _expert_up_kernel(x_ref, wg_ref, wu_ref, h_ref, wgb_ref, wub_ref):
    @pl.when(pl.program_id(2) == 0)
    def _():
        wgb_ref[...] = wg_ref[...].astype(BF16)
        wub_ref[...] = wu_ref[...].astype(BF16)

    x = x_ref[...]
    g = jnp.dot(x, wgb_ref[...], preferred_element_type=F32)
    u = jnp.dot(x, wub_ref[...], preferred_element_type=F32)
    h_ref[...] = (g * jax.nn.sigmoid(g) * u).astype(BF16)


def _expert_down_kernel(h_ref, gate_ref, wd_ref, o_ref, wdb_ref):
    @pl.when(pl.program_id(2) == 0)
    def _():
        wdb_ref[...] = wd_ref[...].astype(BF16)

    y = jnp.dot(h_ref[...], wdb_ref[...], preferred_element_type=F32) * gate_ref[...]
    o_ref[...] = y.astype(BF16)


def _experts(xe, gate, w_gate, w_up, w_down):
    e, cap, d = xe.shape
    ff = w_gate.shape[-1]
    tm = min(1024, cap)
    tf = 512
    tn = 1024
    h = pl.pallas_call(
        _expert_up_kernel,
        grid=(e, ff // tf, cap // tm),
        in_specs=[
            pl.BlockSpec((None, tm, d), lambda i, f, j: (i, j, 0)),
            pl.BlockSpec((None, d, tf), lambda i, f, j: (i, 0, f)),
            pl.BlockSpec((None, d, tf), lambda i, f, j: (i, 0, f)),
        ],
        out_specs=pl.BlockSpec((None, tm, tf), lambda i, f, j: (i, j, f)),
        out_shape=jax.ShapeDtypeStruct((e, cap, ff), BF16),
        scratch_shapes=[pltpu.VMEM((d, tf), BF16), pltpu.VMEM((d, tf), BF16)],
        compiler_params=_cparams("parallel", "arbitrary", "arbitrary"),
        name="expert_up",
    )(xe, w_gate, w_up)
    return pl.pallas_call(
        _expert_down_kernel,
        grid=(e, d // tn, cap // tm),
        in_specs=[
            pl.BlockSpec((None, tm, ff), lambda i, f, j: (i, j, 0)),
            pl.BlockSpec((None, tm, 1), lambda i, f, j: (i, j, 0)),
            pl.BlockSpec((None, ff, tn), lambda i, f, j: (i, 0, f)),
        ],
        out_specs=pl.BlockSpec((None, tm, tn), lambda i, f, j: (i, j, f)),
        out_shape=jax.ShapeDtypeStruct((e, cap, d), BF16),
        scratch_shapes=[pltpu.VMEM((ff, tn), BF16)],
        compiler_params=_cparams("parallel", "arbitrary", "arbitrary"),
        name="expert_down",
    )(h, gate, w_down)


COMBINE_TOKENS = 256
COMBINE_SUBTILES = 2
COMBINE_WIN = 64
WIN_ALIGN = 16
NOT_SELECTED = -(1 << 30)


def _final_kernel(woff_ref, npass_ref, x1_ref, x1b_ref, slot_ref, p_ref, wpp_ref, wpg_ref, g_ref,
                  b_ref, ye_ref, o_ref, win_ref, sem, *, cap, subtiles):
    i = pl.program_id(0)
    t = x1_ref.shape[0] // subtiles
    lane = lax.broadcasted_iota(jnp.int32, (t, 2 * COMBINE_WIN), 1)
    upper = lane >= COMBINE_WIN
    col = lane & (COMBINE_WIN - 1)

    def window_start(h, e, q):
        lo = woff_ref[i * subtiles + h, e] + q * COMBINE_WIN
        return lo, pl.multiple_of(jnp.minimum(lo, cap - COMBINE_WIN), WIN_ALIGN)

    def window_copy(h, e, q):
        _, start = window_start(h, e, q)
        return pltpu.make_async_copy(ye_ref.at[e, pl.ds(start, COMBINE_WIN)],
                                     win_ref.at[h, pl.ds(e * COMBINE_WIN, COMBINE_WIN)], sem.at[h])

    def start_windows(h, q):
        for e in range(N_EXPERTS):
            window_copy(h, e, q).start()

    def wait_windows(h, q):
        for e in range(N_EXPERTS):
            window_copy(h, e, q).wait()

    def routed(h, q):
        slot = slot_ref[h * t:(h + 1) * t, :]
        blocks = []
        for pair in range(N_EXPERTS // 2):
            e0, e1 = 2 * pair, 2 * pair + 1
            lo0, st0 = window_start(h, e0, q)
            lo1, st1 = window_start(h, e1, q)
            rel = jnp.where(upper, slot[:, e1:e1 + 1] - lo1, slot[:, e0:e0 + 1] - lo0)
            shift = jnp.where(upper, lo1 - st1, lo0 - st0)
            hit = (rel >= 0) & (rel < COMBINE_WIN) & (col == rel + shift)
            blocks.append(hit.astype(F32).astype(BF16))
        route = jnp.concatenate(blocks, axis=1)
        return jnp.dot(route, win_ref[h], preferred_element_type=F32)

    for h in range(subtiles):
        start_windows(h, 0)
    bases = []
    for h in range(subtiles):
        rows = slice(h * t, (h + 1) * t)
        pp = jnp.dot(p_ref[rows, :].astype(BF16), wpp_ref[...], preferred_element_type=F32)
        pg = jnp.dot(x1b_ref[rows, :], wpg_ref[...], preferred_element_type=F32)
        bases.append(ALPHA * x1_ref[rows, :] + pp * jax.nn.sigmoid(pg))
    moes = []
    for h in range(subtiles):
        wait_windows(h, 0)
        moes.append(routed(h, 0))
    for h in range(subtiles):
        def extra_pass(q, moe, h=h):
            start_windows(h, q)
            wait_windows(h, q)
            return moe + routed(h, q)

        n_pass = jnp.maximum(npass_ref[i * subtiles + h], 1)
        moe = lax.fori_loop(1, n_pass, extra_pass, moes[h])
        o_ref[h * t:(h + 1) * t, :] = _layer_norm(bases[h] + moe, g_ref[...], b_ref[...])


def _final(x1, x1b, slot_tok, woff, npass, ye, p, w_pp, w_pg, ln_g, ln_b):
    n = x1.shape[0]
    cap = ye.shape[1]
    sub = COMBINE_SUBTILES if n % (COMBINE_SUBTILES * COMBINE_TOKENS) == 0 else 1
    tm = sub * min(COMBINE_TOKENS, n)
    assert cap >= COMBINE_WIN and cap % WIN_ALIGN == 0
    row = lambda w: pl.BlockSpec((tm, w), lambda i, *_: (i, 0))
    return pl.pallas_call(
        functools.partial(_final_kernel, cap=cap, subtiles=sub),
        grid_spec=pltpu.PrefetchScalarGridSpec(
            num_scalar_prefetch=2,
            grid=(n // tm,),
            in_specs=[row(D_MODEL), row(D_MODEL), row(N_EXPERTS), row(PLE_DIM),
                      _resident((PLE_DIM, D_MODEL)), _resident((D_MODEL, D_MODEL)),
                      _resident((1, D_MODEL)), _resident((1, D_MODEL)),
                      pl.BlockSpec(memory_space=pl.ANY)],
            out_specs=row(D_MODEL),
            scratch_shapes=[pltpu.VMEM((sub, N_EXPERTS * COMBINE_WIN, D_MODEL), BF16),
                            pltpu.SemaphoreType.DMA((sub,))],
        ),
        out_shape=jax.ShapeDtypeStruct((n, D_MODEL), F32),
        compiler_params=_cparams("arbitrary"),
        name="ple_combine_final_ln",
    )(woff, npass, x1, x1b, slot_tok, p, w_pp, w_pg, ln_g, ln_b, ye)


def _dispatch_tables(sel3, slot3, n, cap):
    tm = min(COMBINE_TOKENS, n)
    sel = sel3.reshape(N_EXPERTS, n) > 0
    slot = slot3.reshape(N_EXPERTS, n)
    slot_tok = jnp.where(sel, slot, NOT_SELECTED).T
    first = slot[:, ::tm]
    last = jnp.concatenate([first[:, 1:], jnp.full((N_EXPERTS, 1), cap, jnp.int32)], axis=1)
    woff = (first // WIN_ALIGN) * WIN_ALIGN
    passes = jnp.where(last > first, (last - woff + COMBINE_WIN - 1) // COMBINE_WIN, 0)
    return slot_tok, woff.T, jnp.max(passes, axis=0).astype(jnp.int32)


def _trunk(x, p, wts):
    b, s, _ = x.shape
    n = b * s
    x2 = x.reshape(n, D_MODEL)
    proj = _ln_inproj(x2, wts["ln_emb_g"], wts["ln_emb_b"], wts["w_proj"])
    yf = _fourier_branch(proj, wts["g_fourier"], b, s)
    yn = _na_attention(proj.reshape(b, s, D_PROJ), wts["na_bias"], b, s).reshape(n, D_NA)
    x1, x1b, aff = _outproj(yf, yn, x2, wts["ln_emb_g"], wts["ln_emb_b"], wts["g_na"], wts["w_out"],
                            wts["ln1_g"], wts["ln1_b"], wts["w_router"])
    cap = EC_CAPACITY_FACTOR * n // N_EXPERTS
    aff_t = aff[:, :N_EXPERTS].T
    aff3 = aff_t.reshape(N_EXPERTS, n // 128, 128)
    sel3, slot3 = _select(aff3, cap)
    idx, gate = _slot_index(sel3, aff3, cap)
    xe = _gather_rows(x1, idx[:, :, 0])
    ye = _experts(xe, gate[:, :, :1], wts["w_gate"], wts["w_up"], wts["w_down"])
    slot_tok, woff, npass = _dispatch_tables(sel3, slot3, n, cap)
    y = _final(x1, x1b, slot_tok, woff, npass, ye, p.reshape(n, PLE_DIM), wts["w_ple_proj"],
               wts["w_ple_gate"], wts["ln2_g"], wts["ln2_b"])
    return y.reshape(b, s, D_MODEL)


def _prepare_weights(ln_emb_g, ln_emb_b, w_in, w_fourier, rpb, g_fourier, g_na, w_out, ln1_g, ln1_b,
                     w_router, w_gate, w_up, w_down, w_ple_proj, w_ple_gate, ln2_g, ln2_b):
    assert w_in.shape[0] == 1, "one encoder layer"
    w_zr, w_zi = _fold_fourier(w_in[0], w_fourier[0])
    row = lambda a: a.reshape(1, -1).astype(F32)
    return {
        "ln_emb_g": row(ln_emb_g), "ln_emb_b": row(ln_emb_b),
        "w_proj": jnp.concatenate([w_zr, w_zi, w_in[0][:, D_FOURIER:].astype(BF16)], axis=1),
        "na_bias": _na_bias_table(rpb[0]),
        "g_fourier": row(g_fourier[0]), "g_na": row(g_na[0]),
        "w_out": w_out[0].astype(BF16),
        "ln1_g": row(ln1_g[0]), "ln1_b": row(ln1_b[0]),
        "w_router": jnp.pad(w_router[0], ((0, 0), (0, 128 - N_EXPERTS))).astype(BF16),
        "w_gate": w_gate[0], "w_up": w_up[0], "w_down": w_down[0],
        "w_ple_proj": w_ple_proj[0].astype(BF16), "w_ple_gate": w_ple_gate[0].astype(BF16),
        "ln2_g": row(ln2_g[0]), "ln2_b": row(ln2_b[0]),
    }


def kernel(x_prompt, x_sample, p_prompt, p_sample, ln_emb_g, ln_emb_b, w_in, w_fourier, rpb,
           g_fourier, g_na, w_out, ln1_g, ln1_b, w_router, w_gate, w_up, w_down,
           w_ple_proj, w_ple_gate, ln2_g, ln2_b):
    wts = _prepare_weights(ln_emb_g, ln_emb_b, w_in, w_fourier, rpb, g_fourier, g_na, w_out, ln1_g,
                           ln1_b, w_router, w_gate, w_up, w_down, w_ple_proj, w_ple_gate, ln2_g, ln2_b)
    return (_trunk(x_prompt, p_prompt[0], wts), _trunk(x_sample, p_sample[0], wts))
```

```python
import functools

import numpy as np
import jax
import jax.numpy as jnp
from jax import lax
from jax.experimental import pallas as pl
from jax.experimental.pallas import tpu as pltpu

D_MODEL = 2048
GRID_W = 64
D_FOURIER = 1024
FOURIER_GROUPS = 4
FOURIER_GW = 256
D_NA = 1024
NA_HEADS = 16
NA_HEAD_DIM = 64
NA_KH = 8
NA_KW = 16
N_EXPERTS = 16
EC_CAPACITY_FACTOR = 2
PLE_DIM = 256
LN_EPS = 1e-5
RMS_EPS = 1e-6
ALPHA = float(2.0 ** 0.25)
D_PROJ = 2 * D_FOURIER + 3 * D_NA
MASK_VALUE = -1e30
VMEM_LIMIT = 56 * 1024 * 1024
FFT_S1 = 128

BF16 = jnp.bfloat16
F32 = jnp.float32


def _cparams(*sem):
    return pltpu.CompilerParams(dimension_semantics=sem, vmem_limit_bytes=VMEM_LIMIT)


def _resident(shape):
    return pl.BlockSpec(shape, lambda *_: (0,) * len(shape), pipeline_mode=pl.Buffered(1))


def _layer_norm(x, g, b):
    mu = jnp.mean(x, axis=-1, keepdims=True)
    xc = x - mu
    var = jnp.mean(xc * xc, axis=-1, keepdims=True)
    return xc * lax.rsqrt(var + LN_EPS) * g + b


def _rms_norm(x, g):
    return x * lax.rsqrt(jnp.mean(x * x, axis=-1, keepdims=True) + RMS_EPS) * g


def _fold_kernel(win_ref, wf_ref, cc_ref, sc_ref, zr_ref, zi_ref):
    hi = lax.Precision.HIGHEST
    wf = wf_ref[...]
    a = jnp.dot(cc_ref[...], wf, precision=hi, preferred_element_type=F32)
    b = jnp.dot(sc_ref[...], wf, precision=hi, preferred_element_type=F32)
    w = win_ref[...]
    zr_ref[...] = jnp.dot(w, a, precision=hi, preferred_element_type=F32).astype(BF16)
    zi_ref[...] = (-jnp.dot(w, b, precision=hi, preferred_element_type=F32)).astype(BF16)


def _fold_fourier(w_in, w_fourier):
    c = np.arange(FOURIER_GW)
    ang = 2.0 * np.pi * ((c[:, None] * c[None, :]) % FOURIER_GW) / FOURIER_GW
    cc = jnp.asarray(np.cos(ang), F32)
    sc = jnp.asarray(np.sin(ang), F32)
    gw = FOURIER_GW
    return pl.pallas_call(
        _fold_kernel,
        grid=(FOURIER_GROUPS,),
        in_specs=[
            pl.BlockSpec((D_MODEL, gw), lambda g: (0, g)),
            pl.BlockSpec((None, gw, gw), lambda g: (g, 0, 0)),
            pl.BlockSpec((gw, gw), lambda g: (0, 0)),
            pl.BlockSpec((gw, gw), lambda g: (0, 0)),
        ],
        out_specs=[
            pl.BlockSpec((D_MODEL, gw), lambda g: (0, g)),
            pl.BlockSpec((D_MODEL, gw), lambda g: (0, g)),
        ],
        out_shape=[jax.ShapeDtypeStruct((D_MODEL, D_FOURIER), BF16)] * 2,
        compiler_params=_cparams("arbitrary"),
        name="fold_fourier",
    )(w_in, w_fourier, cc, sc)


def _ln_inproj_kernel(x_ref, g_ref, b_ref, w_ref, proj_ref, xn_ref):
    @pl.when(pl.program_id(1) == 0)
    def _():
        xn_ref[...] = _layer_norm(x_ref[...], g_ref[...], b_ref[...]).astype(BF16)

    proj_ref[...] = jnp.dot(xn_ref[...], w_ref[...], preferred_element_type=F32).astype(BF16)


def _ln_inproj(x, g, b, w):
    n = x.shape[0]
    tm = min(1024, n)
    tn = 1024
    return pl.pallas_call(
        _ln_inproj_kernel,
        grid=(n // tm, D_PROJ // tn),
        in_specs=[
            pl.BlockSpec((tm, D_MODEL), lambda i, j: (i, 0)),
            pl.BlockSpec((1, D_MODEL), lambda i, j: (0, 0)),
            pl.BlockSpec((1, D_MODEL), lambda i, j: (0, 0)),
            pl.BlockSpec((D_MODEL, tn), lambda i, j: (0, j)),
        ],
        out_specs=pl.BlockSpec((tm, tn), lambda i, j: (i, j)),
        out_shape=jax.ShapeDtypeStruct((n, D_PROJ), BF16),
        scratch_shapes=[pltpu.VMEM((tm, D_MODEL), BF16)],
        compiler_params=_cparams("parallel", "arbitrary"),
        name="ln_inproj",
    )(x, g, b, w)


def _fft1_kernel(z_ref, m_ref, y_ref, *, tb):
    for t in range(tb):
        x = jnp.concatenate([z_ref[t, :, :D_FOURIER], z_ref[t, :, D_FOURIER:]], axis=0)
        y = jnp.dot(m_ref[t], x, preferred_element_type=F32)
        s1 = y.shape[0] // 2
        y_ref[t, :, :D_FOURIER] = y[:s1].astype(BF16)
        y_ref[t, :, D_FOURIER:] = y[s1:].astype(BF16)


def _fft1_matrices(s1, s2):
    s = s1 * s2
    k1 = np.arange(s1)[None, :, None]
    n = (s2 * np.arange(s1)[None, None, :] + np.arange(s2)[:, None, None])
    ang = 2.0 * np.pi * ((k1 * n) % s) / s
    fr, fi = np.cos(ang), -np.sin(ang)
    m = np.concatenate([np.concatenate([fr, -fi], axis=2),
                        np.concatenate([fi, fr], axis=2)], axis=1)
    return jnp.asarray(m, F32).astype(BF16)


def _fft1(zt, s1, s2):
    b = zt.shape[0]
    tb = min(4, s2)
    m = _fft1_matrices(s1, s2)
    return pl.pallas_call(
        functools.partial(_fft1_kernel, tb=tb),
        grid=(s2 // tb, b),
        in_specs=[
            pl.BlockSpec((None, tb, s1, 2 * D_FOURIER), lambda j, i: (i, j, 0, 0)),
            pl.BlockSpec((tb, 2 * s1, 2 * s1), lambda j, i: (j, 0, 0)),
        ],
        out_specs=pl.BlockSpec((None, tb, s1, 2 * D_FOURIER), lambda j, i: (i, j, 0, 0)),
        out_shape=jax.ShapeDtypeStruct(zt.shape, BF16),
        compiler_params=_cparams("parallel", "parallel"),
        name="fft_stage1",
    )(zt, m)


def _fft2_kernel(v_ref, m_ref, g_ref, o_ref, *, tb):
    m = m_ref[...]
    g = g_ref[...]
    for t in range(tb):
        x = jnp.concatenate([v_ref[t, :, :D_FOURIER], v_ref[t, :, D_FOURIER:]], axis=0)
        y = jnp.dot(m, x, preferred_element_type=F32)
        o_ref[t] = _rms_norm(y, g).astype(BF16)


def _fft2(v, g_fourier, s1, s2):
    b = v.shape[0]
    tb = 8
    k2 = np.arange(s2)
    ang = 2.0 * np.pi * ((k2[:, None] * k2[None, :]) % s2) / s2
    m = jnp.asarray(np.concatenate([np.cos(ang), np.sin(ang)], axis=1), F32).astype(BF16)
    return pl.pallas_call(
        functools.partial(_fft2_kernel, tb=tb),
        grid=(b, s1 // tb),
        in_specs=[
            pl.BlockSpec((None, tb, s2, 2 * D_FOURIER), lambda i, j: (i, j, 0, 0)),
            pl.BlockSpec((s2, 2 * s2), lambda i, j: (0, 0)),
            pl.BlockSpec((1, D_FOURIER), lambda i, j: (0, 0)),
        ],
        out_specs=pl.BlockSpec((None, tb, s2, D_FOURIER), lambda i, j: (i, j, 0, 0)),
        out_shape=jax.ShapeDtypeStruct((b, s1, s2, D_FOURIER), BF16),
        compiler_params=_cparams("parallel", "parallel"),
        name="fft_stage2",
    )(v, m, g_fourier)


def _dft_dense_kernel(z_ref, cs_ref, ss_ref, g_ref, o_ref, *, chunk):
    g = g_ref[...]
    for c in range(o_ref.shape[0] // chunk):
        rows = slice(c * chunk, (c + 1) * chunk)
        y = jnp.dot(cs_ref[rows, :], z_ref[:, :D_FOURIER], preferred_element_type=F32)
        y += jnp.dot(ss_ref[rows, :], z_ref[:, D_FOURIER:], preferred_element_type=F32)
        o_ref[rows, :] = _rms_norm(y, g).astype(BF16)


def _dft_dense(proj3, g_fourier, b, s):
    k = np.arange(s)
    ang = 2.0 * np.pi * ((k[:, None] * k[None, :]) % s) / s
    cs = jnp.asarray(np.cos(ang), F32).astype(BF16)
    ss = jnp.asarray(np.sin(ang), F32).astype(BF16)
    return pl.pallas_call(
        functools.partial(_dft_dense_kernel, chunk=min(512, s)),
        grid=(b,),
        in_specs=[pl.BlockSpec((None, s, 2 * D_FOURIER), lambda i: (i, 0, 0)),
                  _resident((s, s)), _resident((s, s)), _resident((1, D_FOURIER))],
        out_specs=pl.BlockSpec((None, s, D_FOURIER), lambda i: (i, 0, 0)),
        out_shape=jax.ShapeDtypeStruct((b, s, D_FOURIER), BF16),
        compiler_params=_cparams("parallel"),
        name="dft_dense",
    )(proj3, cs, ss, g_fourier).reshape(b * s, D_FOURIER)


DENSE_DFT_MAX_SEQ = 2048


def _fourier_branch(proj, g_fourier, b, s):
    if s <= DENSE_DFT_MAX_SEQ:
        return _dft_dense(proj.reshape(b, s, D_PROJ), g_fourier, b, s)
    s1, s2 = FFT_S1, s // FFT_S1
    z = proj[:, :2 * D_FOURIER].reshape(b, s1, s2, 2 * D_FOURIER)
    y = _fft1(jnp.transpose(z, (0, 2, 1, 3)), s1, s2)
    o = _fft2(jnp.transpose(y, (0, 2, 1, 3)), g_fourier, s1, s2)
    return jnp.transpose(o, (0, 2, 1, 3)).reshape(b * s, D_FOURIER)


NA_ROWS_PER_STEP = 8


def _na_kernel(q_ref, k_ref, v_ref, bias_ref, o_ref, *, rows):
    nk = NA_KH * GRID_W
    lane = lax.broadcasted_iota(jnp.int32, (GRID_W, 2 * NA_HEAD_DIM), 1)
    first = lane < NA_HEAD_DIM
    scale = NA_HEAD_DIM ** -0.5
    head0 = (first.astype(F32) * scale).astype(BF16)
    head1 = ((1.0 - first.astype(F32)) * scale).astype(BF16)

    def body(i, carry):
        rr = [i * NA_ROWS_PER_STEP + u for u in range(NA_ROWS_PER_STEP)]
        rs = [jnp.clip(r - NA_KH // 2, 0, rows - NA_KH) for r in rr]
        tok = lambda r: pl.ds(pl.multiple_of(r * GRID_W, GRID_W), GRID_W)
        win = lambda r: pl.ds(pl.multiple_of(r * GRID_W, GRID_W), nk)
        scores = []
        for r, r0 in zip(rr, rs):
            q = q_ref[tok(r), :]
            qbd = jnp.concatenate([q * head0, q * head1], axis=0)
            s = lax.dot_general(qbd, k_ref[win(r0), :], (((1,), (1,)), ((), ())),
                                preferred_element_type=F32)
            scores.append(s + bias_ref[r - r0])
        probs = []
        for s in scores:
            p = jnp.exp(s - jnp.max(s, axis=-1, keepdims=True))
            probs.append((p.astype(BF16), jnp.sum(p, axis=-1, keepdims=True)))
        for r, r0, (p, l) in zip(rr, rs, probs):
            o = jnp.dot(p, v_ref[win(r0), :], preferred_element_type=F32) / l
            o_ref[tok(r), :] = jnp.where(first, o[:GRID_W], o[GRID_W:]).astype(BF16)
        return carry

    lax.fori_loop(0, rows // NA_ROWS_PER_STEP, body, 0)


def _na_bias_kernel(rpb_ref, o_ref):
    hp = pl.program_id(0)
    shape = (GRID_W, 2 * GRID_W)
    qi = lax.broadcasted_iota(jnp.int32, shape, 0)
    lane = lax.broadcasted_iota(jnp.int32, shape, 1)
    kk = lane & (GRID_W - 1)
    upper = lane >= GRID_W
    coff = jnp.clip(kk - qi, -(NA_KW - 1), NA_KW - 1) + NA_KW - 1
    c0 = jnp.clip(qi - NA_KW // 2, 0, GRID_W - NA_KW)
    valid = (kk >= c0) & (kk < c0 + NA_KW)
    n_ri, n_ci = 2 * NA_KH - 1, 2 * NA_KW - 1
    for hh in range(2):
        head = 2 * hp + hh

        def p_body(p, carry):
            def j_body(j, accs):
                out = []
                for ip in range(NA_KH // 2):
                    base = (head * n_ri + 2 * ip - p + NA_KH - 1) * n_ci + j
                    val = jnp.where(upper, rpb_ref[base + n_ci], rpb_ref[base])
                    out.append(jnp.where(coff == j, val, accs[ip]))
                return tuple(out)

            accs = lax.fori_loop(0, n_ci, j_body, (jnp.zeros(shape, F32),) * (NA_KH // 2))
            for ip in range(NA_KH // 2):
                o_ref[p, hh * GRID_W:(hh + 1) * GRID_W, ip * 128:(ip + 1) * 128] = jnp.where(
                    valid, accs[ip], MASK_VALUE)
            return carry

        lax.fori_loop(0, NA_KH, p_body, 0)


def _na_bias_table(rpb):
    hp = NA_HEADS // 2
    return pl.pallas_call(
        _na_bias_kernel,
        grid=(hp,),
        in_specs=[pl.BlockSpec(memory_space=pltpu.SMEM)],
        out_specs=pl.BlockSpec((None, NA_KH, 2 * GRID_W, NA_KH * GRID_W), lambda h: (h, 0, 0, 0)),
        out_shape=jax.ShapeDtypeStruct((hp, NA_KH, 2 * GRID_W, NA_KH * GRID_W), F32),
        compiler_params=_cparams("parallel"),
        name="na_bias_table",
    )(rpb.astype(F32).reshape(-1))


def _na_attention(proj3, bias, b, s):
    rows = s // GRID_W
    assert rows >= NA_KH and rows % NA_ROWS_PER_STEP == 0
    hp = NA_HEADS // 2
    q0 = 2 * D_FOURIER // 128
    blk = lambda off: pl.BlockSpec((None, s, 128), lambda h, i: (i, 0, off + h))
    return pl.pallas_call(
        functools.partial(_na_kernel, rows=rows),
        grid=(hp, b),
        in_specs=[
            blk(q0), blk(q0 + hp), blk(q0 + 2 * hp),
            pl.BlockSpec((None, NA_KH, 2 * GRID_W, NA_KH * GRID_W), lambda h, i: (h, 0, 0, 0)),
        ],
        out_specs=pl.BlockSpec((None, s, 128), lambda h, i: (i, 0, h)),
        out_shape=jax.ShapeDtypeStruct((b, s, D_NA), BF16),
        compiler_params=_cparams("parallel", "parallel"),
        name="na_attention",
    )(proj3, proj3, proj3, bias)


def _outproj_kernel(yf_ref, yn_ref, x_ref, ge_ref, be_ref, gna_ref, w_ref, g_ref, b_ref, wr_ref,
                    x1_ref, x1b_ref, aff_ref):
    half = x_ref.shape[0] // 2
    parts = [slice(0, half), slice(half, 2 * half)]
    mixes = []
    for rows in parts:
        yn = _rms_norm(yn_ref[rows, :].astype(F32), gna_ref[...]).astype(BF16)
        mix = jnp.dot(yf_ref[rows, :], w_ref[:D_FOURIER, :], preferred_element_type=F32)
        mixes.append(mix + jnp.dot(yn, w_ref[D_FOURIER:, :], preferred_element_type=F32))
    for rows, mix in zip(parts, mixes):
        x0 = _layer_norm(x_ref[rows, :], ge_ref[...], be_ref[...])
        x1 = _layer_norm(ALPHA * x0 + mix, g_ref[...], b_ref[...])
        x1_ref[rows, :] = x1
        x1b = x1.astype(BF16)
        x1b_ref[rows, :] = x1b
        logits = jnp.dot(x1b, wr_ref[...], preferred_element_type=F32)
        lane = lax.broadcasted_iota(jnp.int32, logits.shape, 1)
        logits = jnp.where(lane < N_EXPERTS, logits, MASK_VALUE)
        m = jnp.max(logits, axis=-1, keepdims=True)
        e = jnp.exp(logits - m)
        aff_ref[rows, :] = e / jnp.sum(e, axis=-1, keepdims=True)


def _outproj(yf, yn, x, ln_emb_g, ln_emb_b, g_na, w_out, ln_g, ln_b, w_router):
    n = x.shape[0]
    tm = min(512, n)
    row = lambda w: pl.BlockSpec((tm, w), lambda i: (i, 0))
    return pl.pallas_call(
        _outproj_kernel,
        grid=(n // tm,),
        in_specs=[row(D_FOURIER), row(D_NA), row(D_MODEL), _resident((1, D_MODEL)),
                  _resident((1, D_MODEL)), _resident((1, D_NA)), _resident((D_MODEL, D_MODEL)),
                  _resident((1, D_MODEL)), _resident((1, D_MODEL)), _resident((D_MODEL, 128))],
        out_specs=[row(D_MODEL), row(D_MODEL), row(128)],
        out_shape=[jax.ShapeDtypeStruct((n, D_MODEL), F32),
                   jax.ShapeDtypeStruct((n, D_MODEL), BF16),
                   jax.ShapeDtypeStruct((n, 128), F32)],
        compiler_params=_cparams("parallel"),
        name="outproj_ln_router",
    )(yf, yn, x, ln_emb_g, ln_emb_b, g_na, w_out, ln_g, ln_b, w_router)


def _tri(n, strict, lower):
    r = lax.broadcasted_iota(jnp.int32, (n, n), 0)
    c = lax.broadcasted_iota(jnp.int32, (n, n), 1)
    keep = (c < r if strict else c <= r) if lower else (r < c if strict else r <= c)
    return keep.astype(F32).astype(BF16)


def _prefix_counts(m3):
    e, r, _ = m3.shape
    cin = jnp.dot(m3.reshape(e * r, 128).astype(BF16), _tri(128, False, False),
                  preferred_element_type=F32).reshape(e, r, 128)
    tot = jnp.broadcast_to(cin[:, :, 127:128], (e, r, 128)).astype(BF16)
    low = _tri(r, True, True)
    roff = jnp.stack([jnp.dot(low, tot[i], preferred_element_type=F32) for i in range(e)])
    return cin, roff


def _count(m3):
    return jnp.sum(jnp.sum(m3, axis=1, keepdims=True), axis=2, keepdims=True)


def _select_kernel(aff_ref, sel_ref, slot_ref, *, cap):
    aff = aff_ref[...]
    e = aff.shape[0]

    def bisect(it, thr_bits):
        cand = thr_bits | (jnp.int32(1) << (30 - it))
        n_ge = _count((aff >= pltpu.bitcast(cand, F32)).astype(F32))
        return jnp.where(n_ge >= cap, cand, thr_bits)

    thr = pltpu.bitcast(lax.fori_loop(0, 31, bisect, jnp.zeros((e, 1, 1), jnp.int32)), F32)
    gt = (aff > thr).astype(F32)
    eq = (aff == thr).astype(F32)
    need = cap - _count(gt)
    cin, roff = _prefix_counts(eq)
    eq_rank = roff + cin - eq
    sel = jnp.maximum(gt, eq * (eq_rank < need).astype(F32))
    cin, roff = _prefix_counts(sel)
    sel_ref[...] = sel
    slot_ref[...] = (roff + cin - sel).astype(jnp.int32)


def _select(aff3, cap):
    e, r, _ = aff3.shape
    return pl.pallas_call(
        functools.partial(_select_kernel, cap=cap),
        out_shape=[jax.ShapeDtypeStruct((e, r, 128), F32),
                   jax.ShapeDtypeStruct((e, r, 128), jnp.int32)],
        compiler_params=pltpu.CompilerParams(vmem_limit_bytes=VMEM_LIMIT),
        name="expert_select",
    )(aff3)


def _index_kernel(sel_ref, aff_ref, idx_ref, gate_ref, *, cap):
    m = sel_ref[...]
    r = m.shape[0]
    mb = m.astype(BF16)
    cin = jnp.dot(mb, _tri(128, False, False), preferred_element_type=F32)
    ones = jnp.ones((8, 128), BF16)
    tot_row = lax.dot_general(ones, mb, (((1,), (1,)), ((), ())),
                              preferred_element_type=F32)
    end_row = jnp.dot(tot_row.astype(BF16), _tri(r, False, False),
                      preferred_element_type=F32)[0:1]
    off_row = end_row - tot_row[0:1]
    j = lax.broadcasted_iota(jnp.int32, (cap, r), 0).astype(F32)
    onehot = jnp.logical_and(j >= off_row, j < end_row).astype(F32).astype(BF16)
    row_id = lax.broadcasted_iota(jnp.int32, (r, 128), 0)
    tot = jnp.broadcast_to(cin[:, 127:128], (r, 128)).astype(BF16)
    roff = jnp.dot(_tri(r, True, True), tot, preferred_element_type=F32)
    pick = lambda mat: jnp.dot(onehot, mat.astype(BF16), preferred_element_type=F32)
    cin_j = pick(cin)
    row_j = pick(row_id.astype(F32))
    roff_hi = jnp.floor(roff * (1.0 / 128.0))
    roff_j = pick(roff_hi) * 128.0 + pick(roff - roff_hi * 128.0)
    jloc = lax.broadcasted_iota(jnp.int32, (cap, 128), 0).astype(F32) - roff_j
    lane_j = jnp.dot((cin_j <= jloc).astype(F32).astype(BF16), jnp.ones((128, 128), BF16),
                     preferred_element_type=F32)
    idx_ref[...] = (row_j * 128.0 + lane_j).astype(jnp.int32)
    aff_j = jnp.dot(onehot.astype(F32), aff_ref[...], precision=lax.Precision.HIGHEST,
                    preferred_element_type=F32)
    lane = lax.broadcasted_iota(jnp.int32, (cap, 128), 1).astype(F32)
    gate = jnp.sum(jnp.where(lane == lane_j, aff_j, 0.0), axis=1, keepdims=True)
    gate_ref[...] = jnp.broadcast_to(gate, (cap, 128))


def _slot_index(sel3, aff3, cap):
    e, r, _ = sel3.shape
    blk = pl.BlockSpec((None, r, 128), lambda i: (i, 0, 0))
    out = pl.BlockSpec((None, cap, 128), lambda i: (i, 0, 0))
    return pl.pallas_call(
        functools.partial(_index_kernel, cap=cap),
        grid=(e,),
        in_specs=[blk, blk],
        out_specs=[out, out],
        out_shape=[jax.ShapeDtypeStruct((e, cap, 128), jnp.int32),
                   jax.ShapeDtypeStruct((e, cap, 128), F32)],
        compiler_params=_cparams("parallel"),
        name="expert_slot_index",
    )(sel3, aff3)


GATHER_ROWS = 512


def _gather_kernel(idx_ref, x_ref, o_ref, buf_ref, sem):
    j = pl.program_id(1)
    slot = j % 2
    rows = o_ref.shape[0]

    def row_copy(tile, r, sl):
        tok = idx_ref[0, tile * rows + r]
        return pltpu.make_async_copy(x_ref.at[pl.ds(tok, 1)], buf_ref.at[sl, pl.ds(r, 1)], sem.at[sl])

    def request(tile, sl):
        def body(r, carry):
            row_copy(tile, r, sl).start()
            return carry
        lax.fori_loop(0, rows, body, 0, unroll=32)

    @pl.when(j == 0)
    def _():
        request(0, 0)

    @pl.when(j + 1 < pl.num_programs(1))
    def _():
        request(j + 1, 1 - slot)

    def drain(r, carry):
        row_copy(j, r, slot).wait()
        return carry
    lax.fori_loop(0, rows, drain, 0, unroll=8)
    o_ref[...] = buf_ref[slot].astype(BF16)


def _gather_rows(x1, idx):
    e, cap = idx.shape
    d = x1.shape[1]
    tr = min(GATHER_ROWS, cap)
    assert cap % tr == 0 and tr % 8 == 0
    return pl.pallas_call(
        _gather_kernel,
        grid=(e, cap // tr),
        in_specs=[pl.BlockSpec((None, 1, cap), lambda i, j: (i, 0, 0), memory_space=pltpu.SMEM),
                  pl.BlockSpec(memory_space=pl.ANY)],
        out_specs=pl.BlockSpec((None, tr, d), lambda i, j: (i, j, 0)),
        out_shape=jax.ShapeDtypeStruct((e, cap, d), BF16),
        scratch_shapes=[pltpu.VMEM((2, tr, d), F32), pltpu.SemaphoreType.DMA((2,))],
        compiler_params=_cparams("arbitrary", "arbitrary"),
        name="expert_gather",
    )(idx.reshape(e, 1, cap), x1)


def _expert_up_kernel(x_ref, wg_ref, wu_ref, h_ref, wgb_ref, wub_ref):
    @pl.when(pl.program_id(2) == 0)
    def _():
        wgb_ref[...] = wg_ref[...].astype(BF16)
        wub_ref[...] = wu_ref[...].astype(BF16)

    x = x_ref[...]
    g = jnp.dot(x, wgb_ref[...], preferred_element_type=F32)
    u = jnp.dot(x, wub_ref[...], preferred_element_type=F32)
    h_ref[...] = (g * jax.nn.sigmoid(g) * u).astype(BF16)


def _expert_down_kernel(h_ref, gate_ref, wd_ref, o_ref, wdb_ref):
    @pl.when(pl.program_id(2) == 0)
    def _():
        wdb_ref[...] = wd_ref[...].astype(BF16)

    y = jnp.dot(h_ref[...], wdb_ref[...], preferred_element_type=F32) * gate_ref[...]
    o_ref[...] = y.astype(BF16)


def _experts(xe, gate, w_gate, w_up, w_down):
    e, cap, d = xe.shape
    ff = w_gate.shape[-1]
    tm = min(1024, cap)
    tf = 512
    tn = 1024
    h = pl.pallas_call(
        _expert_up_kernel,
        grid=(e, ff // tf, cap // tm),
        in_specs=[
            pl.BlockSpec((None, tm, d), lambda i, f, j: (i, j, 0)),
            pl.BlockSpec((None, d, tf), lambda i, f, j: (i, 0, f)),
            pl.BlockSpec((None, d, tf), lambda i, f, j: (i, 0, f)),
        ],
        out_specs=pl.BlockSpec((None, tm, tf), lambda i, f, j: (i, j, f)),
        out_shape=jax.ShapeDtypeStruct((e, cap, ff), BF16),
        scratch_shapes=[pltpu.VMEM((d, tf), BF16), pltpu.VMEM((d, tf), BF16)],
        compiler_params=_cparams("parallel", "arbitrary", "arbitrary"),
        name="expert_up",
    )(xe, w_gate, w_up)
    return pl.pallas_call(
        _expert_down_kernel,
        grid=(e, d // tn, cap // tm),
        in_specs=[
            pl.BlockSpec((None, tm, ff), lambda i, f, j: (i, j, 0)),
            pl.BlockSpec((None, tm, 1), lambda i, f, j: (i, j, 0)),
            pl.BlockSpec((None, ff, tn), lambda i, f, j: (i, 0, f)),
        ],
        out_specs=pl.BlockSpec((None, tm, tn), lambda i, f, j: (i, j, f)),
        out_shape=jax.ShapeDtypeStruct((e, cap, d), BF16),
        scratch_shapes=[pltpu.VMEM((ff, tn), BF16)],
        compiler_params=_cparams("parallel", "arbitrary", "arbitrary"),
        name="expert_down",
    )(h, gate, w_down)


COMBINE_TOKENS = 256
COMBINE_SUBTILES = 2
COMBINE_WIN = 64
WIN_ALIGN = 16
NOT_SELECTED = -(1 << 30)


def _final_kernel(woff_ref, npass_ref, x1_ref, x1b_ref, slot_ref, p_ref, wpp_ref, wpg_ref, g_ref,
                  b_ref, ye_ref, o_ref, win_ref, sem, *, cap, subtiles):
    i = pl.program_id(0)
    t = x1_ref.shape[0] // subtiles
    lane = lax.broadcasted_iota(jnp.int32, (t, 2 * COMBINE_WIN), 1)
    upper = lane >= COMBINE_WIN
    col = lane & (COMBINE_WIN - 1)

    def window_start(h, e, q):
        lo = woff_ref[i * subtiles + h, e] + q * COMBINE_WIN
        return lo, pl.multiple_of(jnp.minimum(lo, cap - COMBINE_WIN), WIN_ALIGN)

    def window_copy(h, e, q):
        _, start = window_start(h, e, q)
        return pltpu.make_async_copy(ye_ref.at[e, pl.ds(start, COMBINE_WIN)],
                                     win_ref.at[h, pl.ds(e * COMBINE_WIN, COMBINE_WIN)], sem.at[h])

    def start_windows(h, q):
        for e in range(N_EXPERTS):
            window_copy(h, e, q).start()

    def wait_windows(h, q):
        for e in range(N_EXPERTS):
            window_copy(h, e, q).wait()

    def routed(h, q):
        slot = slot_ref[h * t:(h + 1) * t, :]
        blocks = []
        for pair in range(N_EXPERTS // 2):
            e0, e1 = 2 * pair, 2 * pair + 1
            lo0, st0 = window_start(h, e0, q)
            lo1, st1 = window_start(h, e1, q)
            rel = jnp.where(upper, slot[:, e1:e1 + 1] - lo1, slot[:, e0:e0 + 1] - lo0)
            shift = jnp.where(upper, lo1 - st1, lo0 - st0)
            hit = (rel >= 0) & (rel < COMBINE_WIN) & (col == rel + shift)
            blocks.append(hit.astype(F32).astype(BF16))
        route = jnp.concatenate(blocks, axis=1)
        return jnp.dot(route, win_ref[h], preferred_element_type=F32)

    for h in range(subtiles):
        start_windows(h, 0)
    bases = []
    for h in range(subtiles):
        rows = slice(h * t, (h + 1) * t)
        pp = jnp.dot(p_ref[rows, :].astype(BF16), wpp_ref[...], preferred_element_type=F32)
        pg = jnp.dot(x1b_ref[rows, :], wpg_ref[...], preferred_element_type=F32)
        bases.append(ALPHA * x1_ref[rows, :] + pp * jax.nn.sigmoid(pg))
    moes = []
    for h in range(subtiles):
        wait_windows(h, 0)
        moes.append(routed(h, 0))
    for h in range(subtiles):
        def extra_pass(q, moe, h=h):
            start_windows(h, q)
            wait_windows(h, q)
            return moe + routed(h, q)

        n_pass = jnp.maximum(npass_ref[i * subtiles + h], 1)
        moe = lax.fori_loop(1, n_pass, extra_pass, moes[h])
        o_ref[h * t:(h + 1) * t, :] = _layer_norm(bases[h] + moe, g_ref[...], b_ref[...])


def _final(x1, x1b, slot_tok, woff, npass, ye, p, w_pp, w_pg, ln_g, ln_b):
    n = x1.shape[0]
    cap = ye.shape[1]
    sub = COMBINE_SUBTILES if n % (COMBINE_SUBTILES * COMBINE_TOKENS) == 0 else 1
    tm = sub * min(COMBINE_TOKENS, n)
    assert cap >= COMBINE_WIN and cap % WIN_ALIGN == 0
    row = lambda w: pl.BlockSpec((tm, w), lambda i, *_: (i, 0))
    return pl.pallas_call(
        functools.partial(_final_kernel, cap=cap, subtiles=sub),
        grid_spec=pltpu.PrefetchScalarGridSpec(
            num_scalar_prefetch=2,
            grid=(n // tm,),
            in_specs=[row(D_MODEL), row(D_MODEL), row(N_EXPERTS), row(PLE_DIM),
                      _resident((PLE_DIM, D_MODEL)), _resident((D_MODEL, D_MODEL)),
                      _resident((1, D_MODEL)), _resident((1, D_MODEL)),
                      pl.BlockSpec(memory_space=pl.ANY)],
            out_specs=row(D_MODEL),
            scratch_shapes=[pltpu.VMEM((sub, N_EXPERTS * COMBINE_WIN, D_MODEL), BF16),
                            pltpu.SemaphoreType.DMA((sub,))],
        ),
        out_shape=jax.ShapeDtypeStruct((n, D_MODEL), F32),
        compiler_params=_cparams("arbitrary"),
        name="ple_combine_final_ln",
    )(woff, npass, x1, x1b, slot_tok, p, w_pp, w_pg, ln_g, ln_b, ye)


def _dispatch_tables(sel3, slot3, n, cap):
    tm = min(COMBINE_TOKENS, n)
    sel = sel3.reshape(N_EXPERTS, n) > 0
    slot = slot3.reshape(N_EXPERTS, n)
    slot_tok = jnp.where(sel, slot, NOT_SELECTED).T
    first = slot[:, ::tm]
    last = jnp.concatenate([first[:, 1:], jnp.full((N_EXPERTS, 1), cap, jnp.int32)], axis=1)
    woff = (first // WIN_ALIGN) * WIN_ALIGN
    passes = jnp.where(last > first, (last - woff + COMBINE_WIN - 1) // COMBINE_WIN, 0)
    return slot_tok, woff.T, jnp.max(passes, axis=0).astype(jnp.int32)


def _trunk(x, p, wts):
    b, s, _ = x.shape
    n = b * s
    x2 = x.reshape(n, D_MODEL)
    proj = _ln_inproj(x2, wts["ln_emb_g"], wts["ln_emb_b"], wts["w_proj"])
    yf = _fourier_branch(proj, wts["g_fourier"], b, s)
    yn = _na_attention(proj.reshape(b, s, D_PROJ), wts["na_bias"], b, s).reshape(n, D_NA)
    x1, x1b, aff = _outproj(yf, yn, x2, wts["ln_emb_g"], wts["ln_emb_b"], wts["g_na"], wts["w_out"],
                            wts["ln1_g"], wts["ln1_b"], wts["w_router"])
    cap = EC_CAPACITY_FACTOR * n // N_EXPERTS
    aff_t = aff[:, :N_EXPERTS].T
    aff3 = aff_t.reshape(N_EXPERTS, n // 128, 128)
    sel3, slot3 = _select(aff3, cap)
    idx, gate = _slot_index(sel3, aff3, cap)
    xe = _gather_rows(x1, idx[:, :, 0])
    ye = _experts(xe, gate[:, :, :1], wts["w_gate"], wts["w_up"], wts["w_down"])
    slot_tok, woff, npass = _dispatch_tables(sel3, slot3, n, cap)
    y = _final(x1, x1b, slot_tok, woff, npass, ye, p.reshape(n, PLE_DIM), wts["w_ple_proj"],
               wts["w_ple_gate"], wts["ln2_g"], wts["ln2_b"])
    return y.reshape(b, s, D_MODEL)


def _prepare_weights(ln_emb_g, ln_emb_b, w_in, w_fourier, rpb, g_fourier, g_na, w_out, ln1_g, ln1_b,
                     w_router, w_gate, w_up, w_down, w_ple_proj, w_ple_gate, ln2_g, ln2_b):
    assert w_in.shape[0] == 1, "one encoder layer"
    w_zr, w_zi = _fold_fourier(w_in[0], w_fourier[0])
    row = lambda a: a.reshape(1, -1).astype(F32)
    return {
        "ln_emb_g": row(ln_emb_g), "ln_emb_b": row(ln_emb_b),
        "w_proj": jnp.concatenate([w_zr, w_zi, w_in[0][:, D_FOURIER:].astype(BF16)], axis=1),
        "na_bias": _na_bias_table(rpb[0]),
        "g_fourier": row(g_fourier[0]), "g_na": row(g_na[0]),
        "w_out": w_out[0].astype(BF16),
        "ln1_g": row(ln1_g[0]), "ln1_b": row(ln1_b[0]),
        "w_router": jnp.pad(w_router[0], ((0, 0), (0, 128 - N_EXPERTS))).astype(BF16),
        "w_gate": w_gate[0], "w_up": w_up[0], "w_down": w_down[0],
        "w_ple_proj": w_ple_proj[0].astype(BF16), "w_ple_gate": w_ple_gate[0].astype(BF16),
        "ln2_g": row(ln2_g[0]), "ln2_b": row(ln2_b[0]),
    }


def kernel(x_prompt, x_sample, p_prompt, p_sample, ln_emb_g, ln_emb_b, w_in, w_fourier, rpb,
           g_fourier, g_na, w_out, ln1_g, ln1_b, w_router, w_gate, w_up, w_down,
           w_ple_proj, w_ple_gate, ln2_g, ln2_b):
    wts = _prepare_weights(ln_emb_g, ln_emb_b, w_in, w_fourier, rpb, g_fourier, g_na, w_out, ln1_g,
                           ln1_b, w_router, w_gate, w_up, w_down, w_ple_proj, w_ple_gate, ln2_g, ln2_b)
    return (_trunk(x_prompt, p_prompt[0], wts), _trunk(x_sample, p_sample[0], wts))
```
